```python
import math
import jax, jax.numpy as jnp
from jax import lax
import numpy as np

D_MODEL = 1024
BATCH = 4
SEQ = 8192
DEPTH = 4

N_META = 16
BLOCK = 128
WINDOW = 128
PAD_FRONT = BLOCK - N_META
EPS = 1e-6
N_HEADS_A = 8
N_KV_A = 2
HEAD_DIM_A = 64
N_HEADS_R = 4
HEAD_DIM_R = 128
V_DIM_R = 128
ROPE_BASE = 10000.0
N_BUCKETS = 32
MAX_DISTANCE = 128
N_EXPERTS = 16
CAPACITY_FACTOR = 2
D_FF_EXPERT = 2816
WA_Q = N_HEADS_A * HEAD_DIM_A
WA_KV = N_KV_A * HEAD_DIM_A
WR_QK = N_HEADS_R * HEAD_DIM_R
WR_V = N_HEADS_R * V_DIM_R
SPLITS = (WA_Q, WA_KV, WA_KV, WR_QK, WR_QK, WR_V, WR_V, WR_V)
D_IN = WA_Q + 2 * WA_KV + 2 * WR_QK + 3 * WR_V
D_MIX = WA_Q + WR_V

kernel_name = "hybrid_swa_retention_ec_moe_encoder"


def rms_norm(x, g):
    xf = x.astype(jnp.float32)
    y = xf * lax.rsqrt(jnp.mean(xf * xf, axis=-1, keepdims=True) + EPS)
    return (y * g.astype(jnp.float32)).astype(x.dtype)


def t5_bucket(rel):
    half = N_BUCKETS // 2
    max_exact = half // 2
    n = jnp.abs(rel)
    large = max_exact + (jnp.log(jnp.maximum(n, max_exact).astype(jnp.float32) / max_exact)
                         / math.log(MAX_DISTANCE / max_exact) * (half - max_exact)).astype(jnp.int32)
    large = jnp.minimum(large, half - 1)
    return jnp.where(rel > 0, half, 0) + jnp.where(n < max_exact, n, large)


def neighbour_blocks(t, nb):
    pad = [(0, 0), (BLOCK, BLOCK)] + [(0, 0)] * (t.ndim - 2)
    tp = jnp.pad(t, pad).reshape((t.shape[0], nb + 2, BLOCK) + t.shape[2:])
    return jnp.concatenate([tp[:, :-2], tp[:, 1:-1], tp[:, 2:]], axis=2)


def windowed_gqa(q, k, v, q_g, k_g, sink, rel_bias, key_valid):
    B, Lp = q.shape[0], q.shape[1]
    nb = Lp // BLOCK
    G = N_HEADS_A // N_KV_A
    q = rms_norm(q, q_g)
    k = rms_norm(k, k_g)
    qb = q.reshape(B, nb, BLOCK, N_KV_A, G, HEAD_DIM_A)
    kb = neighbour_blocks(k, nb)
    vb = neighbour_blocks(v, nb)
    validb = neighbour_blocks(key_valid[None], nb)[0]
    rel = (jnp.arange(3 * BLOCK)[None, :] - BLOCK) - jnp.arange(BLOCK)[:, None]
    band = jnp.abs(rel) <= WINDOW
    bias = rel_bias.astype(jnp.float32)[t5_bucket(rel)]
    bias = jnp.transpose(bias, (2, 0, 1)).reshape(N_KV_A, G, BLOCK, 3 * BLOCK)
    mask = band[None] & validb[:, None, :]
    s = jnp.einsum('bnqkgd,bnskd->bnkgqs', qb, kb).astype(jnp.float32) * (HEAD_DIM_A ** -0.5)
    s = jnp.where(mask[None, :, None, None], s + bias[None, None], -1e30)
    sk = sink.astype(jnp.float32).reshape(N_KV_A, G)[None, None, :, :, None, None]
    m = jnp.maximum(jnp.max(s, axis=-1, keepdims=True), sk)
    p = jnp.exp(s - m)
    p = p / (jnp.sum(p, axis=-1, keepdims=True) + jnp.exp(sk - m))
    o = jnp.einsum('bnkgqs,bnskd->bnqkgd', p.astype(v.dtype), vb)
    return o.reshape(B, Lp, N_HEADS_A * HEAD_DIM_A)


def rotary(x, pos):
    d = x.shape[-1]
    inv = ROPE_BASE ** (-jnp.arange(0, d, 2, dtype=jnp.float32) / d)
    ang = pos.astype(jnp.float32)[:, None] * inv[None]
    cos = jnp.cos(ang)[None, :, None, :]
    sin = jnp.sin(ang)[None, :, None, :]
    xf = x.astype(jnp.float32)
    x1, x2 = xf[..., : d // 2], xf[..., d // 2:]
    return jnp.concatenate([x1 * cos - x2 * sin, x1 * sin + x2 * cos], axis=-1).astype(x.dtype)


def chunk_retention(q, k, v, log_gamma):
    B, Lp, H, dk = q.shape
    dv = v.shape[-1]
    nc = Lp // BLOCK
    dt = q.dtype
    lg = log_gamma.astype(jnp.float32)
    idx = jnp.arange(BLOCK, dtype=jnp.float32)
    diff = idx[:, None] - idx[None, :]
    decay_in = jnp.where(diff >= 0, jnp.exp(lg[:, None, None] * jnp.maximum(diff, 0.0)), 0.0).astype(dt)
    xi = jnp.exp(lg[:, None] * (idx + 1.0)).T.astype(dt)
    zeta = jnp.exp(lg[:, None] * (BLOCK - 1.0 - idx)).T.astype(dt)
    chunk_decay = jnp.exp(lg * BLOCK).astype(dt)
    qc = q.reshape(B, nc, BLOCK, H, dk)
    kc = k.reshape(B, nc, BLOCK, H, dk)
    vc = v.reshape(B, nc, BLOCK, H, dv)
    s = jnp.einsum('bnihd,bnjhd->bnhij', qc, kc) * decay_in
    y_in = jnp.einsum('bnhij,bnjhe->bnihe', s, vc)
    kv = jnp.einsum('bnjhd,bnjhe->nbhde', kc * zeta[None, None, :, :, None], vc)

    def step(state, kv_n):
        return state * chunk_decay[None, :, None, None] + kv_n, state

    _, states = lax.scan(step, jnp.zeros(kv.shape[1:], kv.dtype), kv)
    y_cross = jnp.einsum('bnihd,nbhde->bnihe', qc * xi[None, None, :, :, None], states)
    return (y_in + y_cross).reshape(B, Lp, H, dv)


def head_group_norm(y):
    yf = y.astype(jnp.float32)
    mu = jnp.mean(yf, axis=-1, keepdims=True)
    var = jnp.mean(jnp.square(yf - mu), axis=-1, keepdims=True)
    return ((yf - mu) * lax.rsqrt(var + EPS)).astype(y.dtype)


def bidir_retention(q, k, v, g_f, g_b, log_gamma2):
    B, Lp = q.shape[0], q.shape[1]
    flip = lambda t: t[:, ::-1]
    y_f = chunk_retention(q, k, v, log_gamma2[0])
    y_b = flip(chunk_retention(flip(q), flip(k), flip(v), log_gamma2[1]))
    g_f = g_f.reshape(B, Lp, N_HEADS_R, V_DIM_R)
    g_b = g_b.reshape(B, Lp, N_HEADS_R, V_DIM_R)
    o = jax.nn.silu(g_f) * head_group_norm(y_f) + jax.nn.silu(g_b) * head_group_norm(y_b)
    return o.reshape(B, Lp, N_HEADS_R * V_DIM_R)


def hybrid_mixer(h, w_in_l, w_out_l, q_g, k_g, sink_l, rel_bias, ret_decay_l):
    B, L, _ = h.shape
    Lp = L + PAD_FRONT
    proj = jnp.einsum('bld,de->ble', h, w_in_l)
    proj = jnp.pad(proj, ((0, 0), (PAD_FRONT, 0), (0, 0)))
    pos = jnp.arange(Lp)
    valid = pos >= PAD_FRONT
    parts = []
    start = 0
    for w in SPLITS:
        parts.append(proj[..., start:start + w])
        start += w
    aq, ak, av, rq, rk, rv, rg_f, rg_b = parts
    attn = windowed_gqa(aq.reshape(B, Lp, N_HEADS_A, HEAD_DIM_A),
                        ak.reshape(B, Lp, N_KV_A, HEAD_DIM_A),
                        av.reshape(B, Lp, N_KV_A, HEAD_DIM_A),
                        q_g, k_g, sink_l, rel_bias, valid)
    vmask = valid.astype(h.dtype)[None, :, None, None]
    rq = rotary(rq.reshape(B, Lp, N_HEADS_R, HEAD_DIM_R), pos)
    rk = rotary(rk.reshape(B, Lp, N_HEADS_R, HEAD_DIM_R), pos) * (HEAD_DIM_R ** -0.5) * vmask
    rv = rv.reshape(B, Lp, N_HEADS_R, V_DIM_R) * vmask
    log_gamma2 = -jnp.exp(ret_decay_l.astype(jnp.float32))
    ret = bidir_retention(rq, rk, rv, rg_f, rg_b, log_gamma2)
    mixed = jnp.concatenate([attn, ret], axis=-1)[:, PAD_FRONT:]
    return jnp.einsum('ble,ed->bld', mixed, w_out_l)


def expert_choice_ffn(h, w_router, w_gate, w_up, w_down):
    B, L, D = h.shape
    cap = CAPACITY_FACTOR * L // N_EXPERTS
    aff = jax.nn.softmax(jnp.einsum('bld,de->ble', h, w_router).astype(jnp.float32), axis=-1)
    gate, idx = lax.top_k(jnp.swapaxes(aff, 1, 2), cap)
    bidx = jnp.arange(B)[:, None, None]
    xe = jnp.swapaxes(h[bidx, idx], 0, 1)

    def one_expert(args):
        xe_e, wg, wu, wd = args
        a = jnp.einsum('bcd,df->bcf', xe_e, wg)
        u = jnp.einsum('bcd,df->bcf', xe_e, wu)
        return jnp.einsum('bcf,fd->bcd', jax.nn.silu(a) * u, wd)

    ye = jnp.swapaxes(lax.map(one_expert, (xe, w_gate, w_up, w_down)), 0, 1)
    ye = ye * gate[..., None].astype(h.dtype)
    return jnp.zeros_like(h).at[bidx, idx].add(ye)


def setup_inputs(seed: int = 0) -> dict:
    key = jax.random.key(seed)
    ks = jax.random.split(key, 16)
    f32 = jnp.float32
    nrm = lambda k, shape, scale: jax.random.normal(k, shape, f32) * scale
    base = jnp.log(-jnp.log(1.0 - 2.0 ** (-5.0 - jnp.arange(N_HEADS_R, dtype=f32))))
    return {
        "x": nrm(ks[0], (BATCH, SEQ, D_MODEL), 1.0),
        "meta_tokens": nrm(ks[1], (N_META, D_MODEL), 1.0),
        "rel_bias": nrm(ks[2], (N_BUCKETS, N_HEADS_A), 0.5),
        "norm1_g": 1.0 + nrm(ks[3], (DEPTH, D_MODEL), 0.05),
        "w_in": nrm(ks[4], (DEPTH, D_MODEL, D_IN), D_MODEL ** -0.5),
        "q_norm_g": 1.0 + nrm(ks[5], (DEPTH, HEAD_DIM_A), 0.05),
        "k_norm_g": 1.0 + nrm(ks[6], (DEPTH, HEAD_DIM_A), 0.05),
        "attn_sink": nrm(ks[7], (DEPTH, N_HEADS_A), 0.5),
        "ret_decay": jnp.broadcast_to(base, (DEPTH, 2, N_HEADS_R)) + nrm(ks[8], (DEPTH, 2, N_HEADS_R), 0.05),
        "w_out": nrm(ks[9], (DEPTH, D_MIX, D_MODEL), D_MIX ** -0.5),
        "norm2_g": 1.0 + nrm(ks[10], (DEPTH, D_MODEL), 0.05),
        "w_router": nrm(ks[11], (DEPTH, D_MODEL, N_EXPERTS), D_MODEL ** -0.5),
        "w_gate": nrm(ks[12], (DEPTH, N_EXPERTS, D_MODEL, D_FF_EXPERT), D_MODEL ** -0.5),
        "w_up": nrm(ks[13], (DEPTH, N_EXPERTS, D_MODEL, D_FF_EXPERT), D_MODEL ** -0.5),
        "w_down": nrm(ks[14], (DEPTH, N_EXPERTS, D_FF_EXPERT, D_MODEL), D_FF_EXPERT ** -0.5),
    }


def reference(x, meta_tokens, rel_bias, norm1_g, w_in, q_norm_g, k_norm_g, attn_sink, ret_decay,
              w_out, norm2_g, w_router, w_gate, w_up, w_down):
    B = x.shape[0]
    meta = jnp.broadcast_to(meta_tokens.astype(x.dtype)[None], (B, N_META, D_MODEL))
    h = jnp.concatenate([meta, x], axis=1)
    for l in range(DEPTH):
        h = h + hybrid_mixer(rms_norm(h, norm1_g[l]), w_in[l], w_out[l], q_norm_g[l], k_norm_g[l],
                             attn_sink[l], rel_bias, ret_decay[l])
        h = h + expert_choice_ffn(rms_norm(h, norm2_g[l]), w_router[l], w_gate[l], w_up[l], w_down[l])
    return h[:, N_META:]
```

```python
import functools
import math

import jax
import jax.numpy as jnp
from jax import lax
from jax.experimental import pallas as pl
from jax.experimental.pallas import tpu as pltpu

D_MODEL = 1024
N_META = 16
BLOCK = 128
PAD_FRONT = BLOCK - N_META
EPS = 1e-6
N_HEADS_A = 8
N_KV_A = 2
GROUP_A = N_HEADS_A // N_KV_A
HEAD_DIM_A = 64
N_HEADS_R = 4
HEAD_DIM_R = 128
ROPE_BASE = 10000.0
N_BUCKETS = 32
MAX_DISTANCE = 128
N_EXPERTS = 16
CAPACITY_FACTOR = 2
WA_Q = N_HEADS_A * HEAD_DIM_A
WA_KV = N_KV_A * HEAD_DIM_A
WR = N_HEADS_R * HEAD_DIM_R
D_IN = WA_Q + 2 * WA_KV + 5 * WR
NEG = -1e30

LANES = 128
BF16_SUBLANES = 16
VMEM_LIMIT_CAP = 60000 * 1024

F32 = jnp.float32
BF16 = jnp.bfloat16
I32 = jnp.int32


def _params(semantics, vmem_bytes):
    return pltpu.CompilerParams(dimension_semantics=semantics,
                                vmem_limit_bytes=min(int(vmem_bytes), VMEM_LIMIT_CAP))


def _row_tile(lp):
    best = BLOCK
    for t in range(BLOCK, 1024 + 1, BLOCK):
        if lp % t == 0:
            best = t
    return best


def _split_dot(x, w_ref):
    hi = x.astype(BF16)
    lo = (x - hi.astype(F32)).astype(BF16)
    w = w_ref[...]
    return (jnp.dot(hi, w, preferred_element_type=F32) + jnp.dot(lo, w, preferred_element_type=F32))


def _in_proj_kernel(h_ref, g1_ref, w_ref, cc_ref, ss_ref, qg_ref, kg_ref, bdq_ref, bdk_ref,
                    q_ref, k_ref, v_ref, rq_ref, rk_ref, rv_ref, gf_ref, gb_ref, *, tm):
    j = pl.program_id(1)
    x = h_ref[0]
    ms = jnp.mean(x * x, axis=-1, keepdims=True)
    xn = x * lax.rsqrt(ms + EPS) * g1_ref[...]
    row = j * tm + lax.broadcasted_iota(I32, (tm, 1), 0)
    xn = jnp.where(row >= PAD_FRONT, xn, 0.0).astype(BF16)

    def proj(lo, n):
        return jnp.dot(xn, w_ref[:, lo:lo + n], preferred_element_type=F32)

    q = proj(0, WA_Q)
    q_ref[0] = (q * lax.rsqrt(_split_dot(q * q, bdq_ref) + EPS) * qg_ref[...]).astype(BF16)
    k = proj(WA_Q, WA_KV)
    k_ref[0] = (k * lax.rsqrt(_split_dot(k * k, bdk_ref) + EPS) * kg_ref[...]).astype(BF16)
    v_ref[0] = proj(WA_Q + WA_KV, WA_KV).astype(BF16)

    cc = cc_ref[...]
    ss = ss_ref[...]
    base = WA_Q + 2 * WA_KV
    for off, ref, scale in ((base, rq_ref, 1.0), (base + WR, rk_ref, HEAD_DIM_R ** -0.5)):
        r = proj(off, WR)
        for hh in range(N_HEADS_R):
            seg = r[:, hh * HEAD_DIM_R:(hh + 1) * HEAD_DIM_R]
            rot = seg * cc + pltpu.roll(seg, HEAD_DIM_R // 2, 1) * ss
            ref[0, :, hh * HEAD_DIM_R:(hh + 1) * HEAD_DIM_R] = (rot * scale).astype(BF16)
    rv_ref[0] = proj(base + 2 * WR, WR).astype(BF16)
    gf_ref[0] = proj(base + 3 * WR, WR).astype(BF16)
    gb_ref[0] = proj(base + 4 * WR, WR).astype(BF16)


def _in_proj(hp, g1, w_in_bf, cc, ss, qg, kg, bdq, bdk):
    B, Lp, D = hp.shape
    tm = _row_tile(Lp)
    const = lambda b, j: (0, 0)
    row = lambda n: pl.BlockSpec((1, tm, n), lambda b, j: (b, j, 0))
    widths = (WA_Q, WA_KV, WA_KV, WR, WR, WR, WR, WR)
    return pl.pallas_call(
        functools.partial(_in_proj_kernel, tm=tm),
        grid=(B, Lp // tm),
        in_specs=[row(D),
                  pl.BlockSpec((1, D), const),
                  pl.BlockSpec((D, D_IN), const),
                  pl.BlockSpec((tm, HEAD_DIM_R), lambda b, j: (j, 0)),
                  pl.BlockSpec((tm, HEAD_DIM_R), lambda b, j: (j, 0)),
                  pl.BlockSpec((1, WA_Q), const),
                  pl.BlockSpec((1, WA_KV), const),
                  pl.BlockSpec((WA_Q, WA_Q), const),
                  pl.BlockSpec((WA_KV, WA_KV), const)],
        out_specs=[row(n) for n in widths],
        out_shape=[jax.ShapeDtypeStruct((B, Lp, n), BF16) for n in widths],
        compiler_params=_params(("parallel", "parallel"), 48 << 20),
        name="in_proj",
    )(hp, g1, w_in_bf, cc, ss, qg, kg, bdq, bdk)


def _ret_state_kernel(lg_ref, kf_ref, vf_ref, kb_ref, vb_ref, sf_ref, sb_ref, st_ref):
    n = pl.program_id(1)

    @pl.when(n == 0)
    def _():
        st_ref[...] = jnp.zeros_like(st_ref)

    idx = lax.broadcasted_iota(I32, (BLOCK, 1), 0).astype(F32)
    ones_row = jnp.ones((1, HEAD_DIM_R), F32)
    for d, (k_ref, v_ref, s_out) in enumerate(((kf_ref, vf_ref, sf_ref), (kb_ref, vb_ref, sb_ref))):
        for h in range(N_HEADS_R):
            lg = lg_ref[d, h]
            sl = slice(h * HEAD_DIM_R, (h + 1) * HEAD_DIM_R)
            st = st_ref[d, h]
            s_out[0, 0, h] = st.astype(BF16)
            zeta = jnp.exp(lg * ((BLOCK - 1.0 - idx) if d == 0 else idx))
            kz = (k_ref[0, :, sl].astype(F32) * zeta).astype(BF16)
            kv = lax.dot_general(kz, v_ref[0, :, sl], (((0,), (0,)), ((), ())),
                                 preferred_element_type=F32)
            st_ref[d, h] = st * jnp.exp(lg * float(BLOCK) * ones_row) + kv


def _ret_states(lg2, rk, rv):
    B, Lp, _ = rk.shape
    nb = Lp // BLOCK
    fwd = pl.BlockSpec((1, BLOCK, WR), lambda b, n: (b, n, 0))
    bwd = pl.BlockSpec((1, BLOCK, WR), lambda b, n: (b, nb - 1 - n, 0))
    st_shape = (1, 1, N_HEADS_R, HEAD_DIM_R, HEAD_DIM_R)
    return pl.pallas_call(
        _ret_state_kernel,
        grid=(B, nb),
        in_specs=[pl.BlockSpec(memory_space=pltpu.SMEM), fwd, fwd, bwd, bwd],
        out_specs=[pl.BlockSpec(st_shape, lambda b, n: (b, n, 0, 0, 0)),
                   pl.BlockSpec(st_shape, lambda b, n: (b, nb - 1 - n, 0, 0, 0))],
        out_shape=[jax.ShapeDtypeStruct((B, nb) + st_shape[2:], BF16)] * 2,
        scratch_shapes=[pltpu.VMEM((2, N_HEADS_R, HEAD_DIM_R, HEAD_DIM_R), F32)],
        compiler_params=_params(("arbitrary", "arbitrary"), 16 << 20),
        name="ret_state",
    )(lg2, rk, rv, rk, rv)


def _mixer_kernel(lg_ref, sink_ref, q_ref, kp_ref, kc_ref, kn_ref, vp_ref, vc_ref, vn_ref,
                  rq_ref, rk_ref, rv_ref, gf_ref, gb_ref, sf_ref, sb_ref, bias_ref,
                  o_ref, dec_ref, xi_ref, *, lp):
    b = pl.program_id(0)
    n = pl.program_id(1)

    @pl.when((b == 0) & (n == 0))
    def _():
        ii = lax.broadcasted_iota(I32, (BLOCK, BLOCK), 0).astype(F32)
        jj = lax.broadcasted_iota(I32, (BLOCK, BLOCK), 1).astype(F32)
        for d in range(2):
            diff = (ii - jj) if d == 0 else (jj - ii)
            reach = (ii + 1.0) if d == 0 else (float(BLOCK) - ii)
            for h in range(N_HEADS_R):
                lg = lg_ref[d, h]
                dec_ref[d, h] = jnp.where(diff >= 0, jnp.exp(lg * jnp.maximum(diff, 0.0)), 0.0)
                xi_ref[d, h] = jnp.exp(lg * reach)

    kpos = (n - 1) * BLOCK + lax.broadcasted_iota(I32, (1, 3 * BLOCK), 1)
    kvalid = (kpos >= PAD_FRONT) & (kpos < lp)
    for kh in range(N_KV_A):
        sl = slice(kh * HEAD_DIM_A, (kh + 1) * HEAD_DIM_A)
        k3 = jnp.concatenate([kp_ref[0, :, sl], kc_ref[0, :, sl], kn_ref[0, :, sl]], axis=0)
        v3 = jnp.concatenate([vp_ref[0, :, sl], vc_ref[0, :, sl], vn_ref[0, :, sl]], axis=0)
        for g in range(GROUP_A):
            hd = kh * GROUP_A + g
            qh = q_ref[0, :, hd * HEAD_DIM_A:(hd + 1) * HEAD_DIM_A]
            s = lax.dot_general(qh, k3, (((1,), (1,)), ((), ())), preferred_element_type=F32)
            s = jnp.where(kvalid, s + bias_ref[hd], NEG)
            sk = sink_ref[hd]
            m = jnp.maximum(jnp.max(s, axis=-1, keepdims=True), sk)
            p = jnp.exp(s - m)
            denom = jnp.sum(p, axis=-1, keepdims=True) + jnp.exp(sk - m)
            o = jnp.dot(p.astype(BF16), v3, preferred_element_type=F32) / denom
            o_ref[0, :, hd * HEAD_DIM_A:(hd + 1) * HEAD_DIM_A] = o.astype(BF16)

    for h in range(N_HEADS_R):
        sl = slice(h * HEAD_DIM_R, (h + 1) * HEAD_DIM_R)
        qh = rq_ref[0, :, sl]
        kh_ = rk_ref[0, :, sl]
        vh = rv_ref[0, :, sl]
        qk = lax.dot_general(qh, kh_, (((1,), (1,)), ((), ())), preferred_element_type=F32)
        qf = qh.astype(F32)
        acc = None
        for d, (st_ref, g_ref) in enumerate(((sf_ref, gf_ref), (sb_ref, gb_ref))):
            y = jnp.dot((qk * dec_ref[d, h]).astype(BF16), vh, preferred_element_type=F32)
            y = y + jnp.dot((qf * xi_ref[d, h]).astype(BF16), st_ref[0, 0, h],
                            preferred_element_type=F32)
            mu = jnp.mean(y, axis=-1, keepdims=True)
            yc = y - mu
            var = jnp.mean(yc * yc, axis=-1, keepdims=True)
            gate = g_ref[0, :, sl].astype(F32)
            term = gate * jax.nn.sigmoid(gate) * (yc * lax.rsqrt(var + EPS))
            acc = term if acc is None else acc + term
        o_ref[0, :, WA_Q + h * HEAD_DIM_R:WA_Q + (h + 1) * HEAD_DIM_R] = acc.astype(BF16)


def _mixer(lg2, sink, q, k, v, rq, rk, rv, gf, gb, sf, sb, bias):
    B, Lp, _ = q.shape
    nb = Lp // BLOCK
    cur = lambda n_: pl.BlockSpec((1, BLOCK, n_), lambda b, n: (b, n, 0))
    prev = pl.BlockSpec((1, BLOCK, WA_KV), lambda b, n: (b, jnp.maximum(n - 1, 0), 0))
    nxt = pl.BlockSpec((1, BLOCK, WA_KV), lambda b, n: (b, jnp.minimum(n + 1, nb - 1), 0))
    st = pl.BlockSpec((1, 1, N_HEADS_R, HEAD_DIM_R, HEAD_DIM_R), lambda b, n: (b, n, 0, 0, 0))
    smem = pl.BlockSpec(memory_space=pltpu.SMEM)
    tab = pltpu.VMEM((2, N_HEADS_R, BLOCK, HEAD_DIM_R), F32)
    return pl.pallas_call(
        functools.partial(_mixer_kernel, lp=Lp),
        grid=(B, nb),
        in_specs=[smem, smem, cur(WA_Q), prev, cur(WA_KV), nxt, prev, cur(WA_KV), nxt,
                  cur(WR), cur(WR), cur(WR), cur(WR), cur(WR), st, st,
                  pl.BlockSpec((N_HEADS_A, BLOCK, 3 * BLOCK), lambda b, n: (0, 0, 0))],
        out_specs=pl.BlockSpec((1, BLOCK, D_MODEL), lambda b, n: (b, n, 0)),
        out_shape=jax.ShapeDtypeStruct((B, Lp, D_MODEL), BF16),
        scratch_shapes=[tab, tab],
        compiler_params=_params(("arbitrary", "arbitrary"), 32 << 20),
        name="mixer",
    )(lg2, sink, q, k, k, k, v, v, v, rq, rk, rv, gf, gb, sf, sb, bias)


def _out_proj_kernel(mix_ref, h_ref, w_ref, g2_ref, wr_hi_ref, wr_lo_ref,
                     h1_ref, hn_ref, aff_ref, *, tm):
    j = pl.program_id(1)
    h1 = h_ref[0] + jnp.dot(mix_ref[0], w_ref[...], preferred_element_type=F32)
    h1_ref[0] = h1
    ms = jnp.mean(h1 * h1, axis=-1, keepdims=True)
    xn = h1 * lax.rsqrt(ms + EPS) * g2_ref[...]
    hn_ref[0] = xn.astype(BF16)
    hi = xn.astype(BF16)
    lo = (xn - hi.astype(F32)).astype(BF16)
    logits = (jnp.dot(hi, wr_hi_ref[...], preferred_element_type=F32)
              + jnp.dot(hi, wr_lo_ref[...], preferred_element_type=F32)
              + jnp.dot(lo, wr_hi_ref[...], preferred_element_type=F32))
    lt = logits.T[:N_EXPERTS]
    mx = jnp.max(lt, axis=0, keepdims=True)
    ex = jnp.exp(lt - mx)
    aff = ex / jnp.sum(ex, axis=0, keepdims=True)
    col = j * tm + lax.broadcasted_iota(I32, (1, tm), 1)
    aff_ref[0] = jnp.where(col >= PAD_FRONT, aff, -1.0)


def _out_proj(mixed, hp, w_out_bf, g2, wr_hi, wr_lo):
    B, Lp, D = hp.shape
    tm = _row_tile(Lp)
    const = lambda b, j: (0, 0)
    row = lambda: pl.BlockSpec((1, tm, D), lambda b, j: (b, j, 0))
    return pl.pallas_call(
        functools.partial(_out_proj_kernel, tm=tm),
        grid=(B, Lp // tm),
        in_specs=[row(), row(), pl.BlockSpec((D, D), const), pl.BlockSpec((1, D), const),
                  pl.BlockSpec((D, LANES), const), pl.BlockSpec((D, LANES), const)],
        out_specs=[row(), row(), pl.BlockSpec((1, N_EXPERTS, tm), lambda b, j: (b, 0, j))],
        out_shape=[jax.ShapeDtypeStruct((B, Lp, D), F32),
                   jax.ShapeDtypeStruct((B, Lp, D), BF16),
                   jax.ShapeDtypeStruct((B, N_EXPERTS, Lp), F32)],
        compiler_params=_params(("parallel", "parallel"), 32 << 20),
        name="out_proj",
    )(mixed, hp, w_out_bf, g2, wr_hi, wr_lo)


def _route_kernel(aff_ref, posm_ref, cpos_ref, *, cap, nb):
    E = N_EXPERTS

    def keys(sl=slice(None)):
        return pltpu.bitcast(aff_ref[0, :, sl], I32)

    def count_ge(t):
        return jnp.sum((keys() >= t).astype(I32), axis=-1, keepdims=True)

    def search(_, carry):
        lo, hi = carry
        mid = lo + (hi - lo) // 2
        ok = count_ge(mid) >= cap
        return jnp.where(ok, mid, lo), jnp.where(ok, hi, mid)

    lo0 = jnp.zeros((E, 1), I32)
    hi0 = jnp.full((E, 1), 0x3F800001, I32)
    thr, _ = lax.fori_loop(0, 31, search, (lo0, hi0))
    n_gt = jnp.sum((keys() > thr).astype(F32), axis=-1, keepdims=True)
    need_eq = float(cap) - n_gt

    jr = lax.broadcasted_iota(I32, (LANES, LANES), 0)
    jc = lax.broadcasted_iota(I32, (LANES, LANES), 1)
    upper = (jr < jc).astype(F32).astype(BF16)
    lane = lax.broadcasted_iota(I32, (E, LANES), 1)

    def chunk(c, carry):
        cg, ce, cp = carry
        off = pl.multiple_of(c * LANES, LANES)
        kk = pltpu.bitcast(aff_ref[0, :, pl.ds(off, LANES)], I32)
        gt = kk > thr
        eq = kk == thr
        both = jnp.concatenate([gt, eq], axis=0).astype(F32)
        ex = jnp.dot(both.astype(BF16), upper, preferred_element_type=F32)
        exg = ex[:E] + cg
        exe = ex[E:] + ce
        sel = gt | (eq & (exe < need_eq))
        pos = exg + jnp.minimum(exe, need_eq)
        posm_ref[0, :, pl.ds(off, LANES)] = jnp.where(sel, pos, -1.0).astype(I32)
        cp = jnp.where(lane == c, (cg + jnp.minimum(ce, need_eq)).astype(I32), cp)
        tot = jnp.sum(both, axis=-1, keepdims=True)
        return cg + tot[:E], ce + tot[E:], cp

    zero = jnp.zeros((E, 1), F32)
    _, _, cp = lax.fori_loop(0, nb, chunk, (zero, zero, jnp.zeros((E, LANES), I32)))
    cpos_ref[0] = cp


def _route(aff_t, cap):
    B, E, Lp = aff_t.shape
    nb = Lp // LANES
    assert nb <= LANES
    return pl.pallas_call(
        functools.partial(_route_kernel, cap=cap, nb=nb),
        grid=(B,),
        in_specs=[pl.BlockSpec((1, E, Lp), lambda b: (b, 0, 0))],
        out_specs=[pl.BlockSpec((1, E, Lp), lambda b: (b, 0, 0)),
                   pl.BlockSpec((1, E, LANES), lambda b: (b, 0, 0))],
        out_shape=[jax.ShapeDtypeStruct((B, E, Lp), I32),
                   jax.ShapeDtypeStruct((B, E, LANES), I32)],
        compiler_params=_params(("parallel",), 16 << 20),
        name="route",
    )(aff_t)


def _window_start(cpos_ref, b, e, c, w, cpad):
    cp = cpos_ref[(b * N_EXPERTS + e) * LANES + c]
    a0 = jnp.minimum((cp // BF16_SUBLANES) * BF16_SUBLANES, cpad - w)
    return pl.multiple_of(a0, BF16_SUBLANES)


EXPERT_GROUP = 4


def _gather_kernel(cpos_ref, hn_ref, posm_ref, x_ref, *, ch, w, cpad):
    b = pl.program_id(0)
    g = pl.program_id(1)
    c = pl.program_id(2)

    @pl.when(c == 0)
    def _():
        x_ref[...] = jnp.zeros_like(x_ref)

    rows = lax.broadcasted_iota(I32, (w, ch), 0)
    starts = []
    onehots = []
    for e in range(EXPERT_GROUP):
        a0 = _window_start(cpos_ref, b, g * EXPERT_GROUP + e, c, w, cpad)
        starts.append(a0)
        onehots.append((rows + a0 == posm_ref[0, 0, e:e + 1, :]).astype(F32).astype(BF16))
    res = jnp.dot(jnp.concatenate(onehots, axis=0), hn_ref[0], preferred_element_type=F32)
    for e in range(EXPERT_GROUP):
        win = x_ref[0, e, pl.ds(starts[e], w), :].astype(F32) + res[e * w:(e + 1) * w]
        x_ref[0, e, pl.ds(starts[e], w), :] = win.astype(BF16)


def _gather(cpos_flat, hn, posm, ch, w, cpad):
    B, Lp, D = hn.shape
    E = N_EXPERTS
    ng = E // EXPERT_GROUP
    posm4 = posm.reshape(B, ng, EXPERT_GROUP, Lp)
    grid_spec = pltpu.PrefetchScalarGridSpec(
        num_scalar_prefetch=1,
        grid=(B, ng, Lp // ch),
        in_specs=[pl.BlockSpec((1, ch, D), lambda b, g, c, cp: (b, c, 0)),
                  pl.BlockSpec((1, 1, EXPERT_GROUP, ch), lambda b, g, c, cp: (b, g, 0, c))],
        out_specs=pl.BlockSpec((1, EXPERT_GROUP, cpad, D), lambda b, g, c, cp: (b, g, 0, 0)),
    )
    return pl.pallas_call(
        functools.partial(_gather_kernel, ch=ch, w=w, cpad=cpad),
        grid_spec=grid_spec,
        out_shape=jax.ShapeDtypeStruct((B, E, cpad, D), BF16),
        compiler_params=_params(("arbitrary", "arbitrary", "arbitrary"), 40 << 20),
        name="gather",
    )(cpos_flat, hn, posm4)


FFN_TILE = 256
FFN_BATCH = 2


def _ffn_kernel(x_ref, wg_ref, wu_ref, wd_ref, y_ref, acc_ref):
    f = pl.program_id(2)
    rows = acc_ref.shape[0]
    x = x_ref[...].reshape(rows, D_MODEL)
    a = jnp.dot(x, wg_ref[0, 0].astype(BF16), preferred_element_type=F32)
    u = jnp.dot(x, wu_ref[0, 0].astype(BF16), preferred_element_type=F32)
    hm = (a * jax.nn.sigmoid(a) * u).astype(BF16)
    part = jnp.dot(hm, wd_ref[0, 0].astype(BF16), preferred_element_type=F32)

    @pl.when(f == 0)
    def _():
        acc_ref[...] = part

    @pl.when(f > 0)
    def _():
        acc_ref[...] += part

    @pl.when(f == pl.num_programs(2) - 1)
    def _():
        y_ref[...] = acc_ref[...].astype(BF16).reshape(y_ref.shape)


def _ffn(xe, w_gate, w_up, w_down, layer):
    B, E, cpad, D = xe.shape
    F = w_gate.shape[-1]
    tf = FFN_TILE if F % FFN_TILE == 0 else LANES
    bh = FFN_BATCH if B % FFN_BATCH == 0 else 1
    xspec = pl.BlockSpec((bh, 1, cpad, D), lambda e, b, f: (b, e, 0, 0))
    return pl.pallas_call(
        _ffn_kernel,
        grid=(E, B // bh, F // tf),
        in_specs=[xspec,
                  pl.BlockSpec((1, 1, D, tf), lambda e, b, f: (layer, e, 0, f)),
                  pl.BlockSpec((1, 1, D, tf), lambda e, b, f: (layer, e, 0, f)),
                  pl.BlockSpec((1, 1, tf, D), lambda e, b, f: (layer, e, f, 0))],
        out_specs=xspec,
        out_shape=jax.ShapeDtypeStruct(xe.shape, BF16),
        scratch_shapes=[pltpu.VMEM((bh * cpad, D), F32)],
        compiler_params=_params(("parallel", "parallel", "arbitrary"), 56 << 20),
        name="ffn",
    )(xe, w_gate, w_up, w_down)


def _combine_kernel(cpos_ref, h_ref, posm_ref, gate_ref, ye_ref, o_ref, *, ch, w, cpad):
    b = pl.program_id(0)
    c = pl.program_id(1)
    rows = lax.broadcasted_iota(I32, (w, ch), 0)
    weights = []
    windows = []
    for e in range(N_EXPERTS):
        a0 = _window_start(cpos_ref, b, e, c, w, cpad)
        hit = rows + a0 == posm_ref[0, e:e + 1, :]
        weights.append(jnp.where(hit, gate_ref[0, e:e + 1, :], 0.0).astype(BF16))
        windows.append(ye_ref[0, e, pl.ds(a0, w), :])
    moe = lax.dot_general(jnp.concatenate(weights, axis=0), jnp.concatenate(windows, axis=0),
                          (((0,), (0,)), ((), ())), preferred_element_type=F32)
    o_ref[0] = h_ref[0] + moe


def _combine(cpos_flat, h1, posm, gate, ye, ch, w, cpad):
    B, Lp, D = h1.shape
    E = N_EXPERTS
    grid_spec = pltpu.PrefetchScalarGridSpec(
        num_scalar_prefetch=1,
        grid=(B, Lp // ch),
        in_specs=[pl.BlockSpec((1, ch, D), lambda b, c, cp: (b, c, 0)),
                  pl.BlockSpec((1, E, ch), lambda b, c, cp: (b, 0, c)),
                  pl.BlockSpec((1, E, ch), lambda b, c, cp: (b, 0, c)),
                  pl.BlockSpec((1, E, cpad, D), lambda b, c, cp: (b, 0, 0, 0),
                               pipeline_mode=pl.Buffered(1))],
        out_specs=pl.BlockSpec((1, ch, D), lambda b, c, cp: (b, c, 0)),
    )
    return pl.pallas_call(
        functools.partial(_combine_kernel, ch=ch, w=w, cpad=cpad),
        grid_spec=grid_spec,
        out_shape=jax.ShapeDtypeStruct((B, Lp, D), F32),
        compiler_params=_params(("arbitrary", "arbitrary"), VMEM_LIMIT_CAP),
        name="combine",
    )(cpos_flat, h1, posm, gate, ye)


def _t5_bucket(rel):
    half = N_BUCKETS // 2
    max_exact = half // 2
    n = jnp.abs(rel)
    large = max_exact + (jnp.log(jnp.maximum(n, max_exact).astype(F32) / max_exact)
                         / math.log(MAX_DISTANCE / max_exact) * (half - max_exact)).astype(I32)
    large = jnp.minimum(large, half - 1)
    return jnp.where(rel > 0, half, 0) + jnp.where(n < max_exact, n, large)


def _block_diag_mean(n, group):
    i = jnp.arange(n)
    return jnp.where((i[:, None] // group) == (i[None, :] // group), 1.0 / group, 0.0).astype(BF16)


def kernel(x, meta_tokens, rel_bias, norm1_g, w_in, q_norm_g, k_norm_g, attn_sink, ret_decay,
           w_out, norm2_g, w_router, w_gate, w_up, w_down):
    B, seq, D = x.shape
    depth = w_in.shape[0]
    L = seq + N_META
    Lp = L + PAD_FRONT
    assert D == D_MODEL and Lp % BLOCK == 0
    cap = CAPACITY_FACTOR * L // N_EXPERTS
    ch = BLOCK
    w = ch + BF16_SUBLANES
    cpad = -(-max(cap, w) // BF16_SUBLANES) * BF16_SUBLANES

    hp = jnp.concatenate([jnp.zeros((B, PAD_FRONT, D), x.dtype),
                          jnp.broadcast_to(meta_tokens.astype(x.dtype)[None], (B, N_META, D)), x], axis=1)

    pos = jnp.arange(Lp, dtype=F32)
    inv = ROPE_BASE ** (-jnp.arange(0, HEAD_DIM_R, 2, dtype=F32) / HEAD_DIM_R)
    ang = pos[:, None] * inv[None]
    cc = jnp.concatenate([jnp.cos(ang), jnp.cos(ang)], axis=-1)
    ss = jnp.concatenate([-jnp.sin(ang), jnp.sin(ang)], axis=-1)
    rel = (jnp.arange(3 * BLOCK)[None, :] - BLOCK) - jnp.arange(BLOCK)[:, None]
    bias = jnp.transpose(rel_bias.astype(F32)[_t5_bucket(rel)], (2, 0, 1))
    bias = jnp.where((jnp.abs(rel) <= BLOCK)[None], bias, NEG)
    bdq = _block_diag_mean(WA_Q, HEAD_DIM_A)
    bdk = _block_diag_mean(WA_KV, HEAD_DIM_A)

    for l in range(depth):
        qg = (jnp.tile(q_norm_g[l].astype(F32), N_HEADS_A) * HEAD_DIM_A ** -0.5)[None]
        kg = jnp.tile(k_norm_g[l].astype(F32), N_KV_A)[None]
        lg2 = -jnp.exp(ret_decay[l].astype(F32))
        q, k, v, rq, rk, rv, gf, gb = _in_proj(hp, norm1_g[l][None].astype(F32), w_in[l].astype(BF16),
                                               cc, ss, qg, kg, bdq, bdk)
        sf, sb = _ret_states(lg2, rk, rv)
        mixed = _mixer(lg2, attn_sink[l].astype(F32), q, k, v, rq, rk, rv, gf, gb, sf, sb, bias)
        wr = jnp.pad(w_router[l].astype(F32), ((0, 0), (0, LANES - N_EXPERTS)))
        wr_hi = wr.astype(BF16)
        wr_lo = (wr - wr_hi.astype(F32)).astype(BF16)
        h1, hn, aff = _out_proj(mixed, hp, w_out[l].astype(BF16), norm2_g[l][None].astype(F32), wr_hi, wr_lo)
        posm, cpos = _route(aff, cap)
        cpos_flat = cpos.reshape(-1)
        xe = _gather(cpos_flat, hn, posm, ch, w, cpad)
        ye = _ffn(xe, w_gate, w_up, w_down, l)
        hp = _combine(cpos_flat, h1, posm, aff, ye, ch, w, cpad)
    return hp[:, PAD_FRONT + N_META:]
```

```python
import functools
import math

import jax
import jax.numpy as jnp
from jax import lax
from jax.experimental import pallas as pl
from jax.experimental.pallas import tpu as pltpu

D_MODEL = 1024
N_META = 16
BLOCK = 128
PAD_FRONT = BLOCK - N_META
EPS = 1e-6
N_HEADS_A = 8
N_KV_A = 2
GROUP_A = N_HEADS_A // N_KV_A
HEAD_DIM_A = 64
N_HEADS_R = 4
HEAD_DIM_R = 128
ROPE_BASE = 10000.0
N_BUCKETS = 32
MAX_DISTANCE = 128
N_EXPERTS = 16
CAPACITY_FACTOR = 2
WA_Q = N_HEADS_A * HEAD_DIM_A
WA_KV = N_KV_A * HEAD_DIM_A
WR = N_HEADS_R * HEAD_DIM_R
D_IN = WA_Q + 2 * WA_KV + 5 * WR
NEG = -1e30

LANES = 128
BF16_SUBLANES = 16
VMEM_LIMIT_CAP = 60000 * 1024

F32 = jnp.float32
BF16 = jnp.bfloat16
I32 = jnp.int32


def _params(semantics, vmem_bytes):
    return pltpu.CompilerParams(dimension_semantics=semantics,
                                vmem_limit_bytes=min(int(vmem_bytes), VMEM_LIMIT_CAP))


def _row_tile(lp):
    best = BLOCK
    for t in range(BLOCK, 1024 + 1, BLOCK):
        if lp % t == 0:
            best = t
    return best


def _split_dot(x, w_ref):
    hi = x.astype(BF16)
    lo = (x - hi.astype(F32)).astype(BF16)
    w = w_ref[...]
    return (jnp.dot(hi, w, preferred_element_type=F32) + jnp.dot(lo, w, preferred_element_type=F32))


def _in_proj_kernel(h_ref, g1_ref, w_ref, cc_ref, ss_ref, qg_ref, kg_ref, bdq_ref, bdk_ref,
                    q_ref, k_ref, v_ref, rq_ref, rk_ref, rv_ref, gf_ref, gb_ref, *, tm):
    j = pl.program_id(1)
    x = h_ref[0]
    ms = jnp.mean(x * x, axis=-1, keepdims=True)
    xn = x * lax.rsqrt(ms + EPS) * g1_ref[...]
    row = j * tm + lax.broadcasted_iota(I32, (tm, 1), 0)
    xn = jnp.where(row >= PAD_FRONT, xn, 0.0).astype(BF16)

    def proj(lo, n):
        return jnp.dot(xn, w_ref[:, lo:lo + n], preferred_element_type=F32)

    q = proj(0, WA_Q)
    q_ref[0] = (q * lax.rsqrt(_split_dot(q * q, bdq_ref) + EPS) * qg_ref[...]).astype(BF16)
    k = proj(WA_Q, WA_KV)
    k_ref[0] = (k * lax.rsqrt(_split_dot(k * k, bdk_ref) + EPS) * kg_ref[...]).astype(BF16)
    v_ref[0] = proj(WA_Q + WA_KV, WA_KV).astype(BF16)

    cc = cc_ref[...]
    ss = ss_ref[...]
    base = WA_Q + 2 * WA_KV
    for off, ref, scale in ((base, rq_ref, 1.0), (base + WR, rk_ref, HEAD_DIM_R ** -0.5)):
        r = proj(off, WR)
        for hh in range(N_HEADS_R):
            seg = r[:, hh * HEAD_DIM_R:(hh + 1) * HEAD_DIM_R]
            rot = seg * cc + pltpu.roll(seg, HEAD_DIM_R // 2, 1) * ss
            ref[0, :, hh * HEAD_DIM_R:(hh + 1) * HEAD_DIM_R] = (rot * scale).astype(BF16)
    rv_ref[0] = proj(base + 2 * WR, WR).astype(BF16)
    gf_ref[0] = proj(base + 3 * WR, WR).astype(BF16)
    gb_ref[0] = proj(base + 4 * WR, WR).astype(BF16)


def _in_proj(hp, g1, w_in_bf, cc, ss, qg, kg, bdq, bdk):
    B, Lp, D = hp.shape
    tm = _row_tile(Lp)
    const = lambda b, j: (0, 0)
    row = lambda n: pl.BlockSpec((1, tm, n), lambda b, j: (b, j, 0))
    widths = (WA_Q, WA_KV, WA_KV, WR, WR, WR, WR, WR)
    return pl.pallas_call(
        functools.partial(_in_proj_kernel, tm=tm),
        grid=(B, Lp // tm),
        in_specs=[row(D),
                  pl.BlockSpec((1, D), const),
                  pl.BlockSpec((D, D_IN), const),
                  pl.BlockSpec((tm, HEAD_DIM_R), lambda b, j: (j, 0)),
                  pl.BlockSpec((tm, HEAD_DIM_R), lambda b, j: (j, 0)),
                  pl.BlockSpec((1, WA_Q), const),
                  pl.BlockSpec((1, WA_KV), const),
                  pl.BlockSpec((WA_Q, WA_Q), const),
                  pl.BlockSpec((WA_KV, WA_KV), const)],
        out_specs=[row(n) for n in widths],
        out_shape=[jax.ShapeDtypeStruct((B, Lp, n), BF16) for n in widths],
        compiler_params=_params(("parallel", "parallel"), 48 << 20),
        name="in_proj",
    )(hp, g1, w_in_bf, cc, ss, qg, kg, bdq, bdk)


def _ret_state_kernel(lg_ref, kf_ref, vf_ref, kb_ref, vb_ref, sf_ref, sb_ref, st_ref):
    n = pl.program_id(0)

    @pl.when(n == 0)
    def _():
        st_ref[...] = jnp.zeros_like(st_ref)

    idx = lax.broadcasted_iota(I32, (BLOCK, 1), 0).astype(F32)
    ones_row = jnp.ones((1, HEAD_DIM_R), F32)
    for d, (k_ref, v_ref, s_out) in enumerate(((kf_ref, vf_ref, sf_ref), (kb_ref, vb_ref, sb_ref))):
        for h in range(N_HEADS_R):
            lg = lg_ref[d, h]
            sl = slice(h * HEAD_DIM_R, (h + 1) * HEAD_DIM_R)
            zeta = jnp.exp(lg * ((BLOCK - 1.0 - idx) if d == 0 else idx))
            chunk_decay = jnp.exp(lg * float(BLOCK) * ones_row)
            for b in range(st_ref.shape[0]):
                st = st_ref[b, d, h]
                s_out[b, 0, h] = st.astype(BF16)
                kz = (k_ref[b, :, sl].astype(F32) * zeta).astype(BF16)
                kv = lax.dot_general(kz, v_ref[b, :, sl], (((0,), (0,)), ((), ())),
                                     preferred_element_type=F32)
                st_ref[b, d, h] = st * chunk_decay + kv


def _ret_states(lg2, rk, rv):
    B, Lp, _ = rk.shape
    nb = Lp // BLOCK
    fwd = pl.BlockSpec((B, BLOCK, WR), lambda n: (0, n, 0))
    bwd = pl.BlockSpec((B, BLOCK, WR), lambda n: (0, nb - 1 - n, 0))
    st_shape = (B, 1, N_HEADS_R, HEAD_DIM_R, HEAD_DIM_R)
    return pl.pallas_call(
        _ret_state_kernel,
        grid=(nb,),
        in_specs=[pl.BlockSpec(memory_space=pltpu.SMEM), fwd, fwd, bwd, bwd],
        out_specs=[pl.BlockSpec(st_shape, lambda n: (0, n, 0, 0, 0)),
                   pl.BlockSpec(st_shape, lambda n: (0, nb - 1 - n, 0, 0, 0))],
        out_shape=[jax.ShapeDtypeStruct((B, nb) + st_shape[2:], BF16)] * 2,
        scratch_shapes=[pltpu.VMEM((B, 2, N_HEADS_R, HEAD_DIM_R, HEAD_DIM_R), F32)],
        compiler_params=_params(("arbitrary",), 24 << 20),
        name="ret_state",
    )(lg2, rk, rv, rk, rv)


def _mixer_kernel(lg_ref, sink_ref, q_ref, kp_ref, kc_ref, kn_ref, vp_ref, vc_ref, vn_ref,
                  rq_ref, rk_ref, rv_ref, gf_ref, gb_ref, sf_ref, sb_ref, bias_ref,
                  o_ref, dec_ref, xi_ref, *, lp):
    b = pl.program_id(0)
    n = pl.program_id(1)

    @pl.when((b == 0) & (n == 0))
    def _():
        ii = lax.broadcasted_iota(I32, (BLOCK, BLOCK), 0).astype(F32)
        jj = lax.broadcasted_iota(I32, (BLOCK, BLOCK), 1).astype(F32)
        for d in range(2):
            diff = (ii - jj) if d == 0 else (jj - ii)
            reach = (ii + 1.0) if d == 0 else (float(BLOCK) - ii)
            for h in range(N_HEADS_R):
                lg = lg_ref[d, h]
                dec_ref[d, h] = jnp.where(diff >= 0, jnp.exp(lg * jnp.maximum(diff, 0.0)), 0.0)
                xi_ref[d, h] = jnp.exp(lg * reach)

    kpos = (n - 1) * BLOCK + lax.broadcasted_iota(I32, (1, 3 * BLOCK), 1)
    kvalid = (kpos >= PAD_FRONT) & (kpos < lp)
    for kh in range(N_KV_A):
        sl = slice(kh * HEAD_DIM_A, (kh + 1) * HEAD_DIM_A)
        k3 = jnp.concatenate([kp_ref[0, :, sl], kc_ref[0, :, sl], kn_ref[0, :, sl]], axis=0)
        v3 = jnp.concatenate([vp_ref[0, :, sl], vc_ref[0, :, sl], vn_ref[0, :, sl]], axis=0)
        for g in range(GROUP_A):
            hd = kh * GROUP_A + g
            qh = q_ref[0, :, hd * HEAD_DIM_A:(hd + 1) * HEAD_DIM_A]
            s = lax.dot_general(qh, k3, (((1,), (1,)), ((), ())), preferred_element_type=F32)
            s = jnp.where(kvalid, s + bias_ref[hd], NEG)
            sk = sink_ref[hd]
            m = jnp.maximum(jnp.max(s, axis=-1, keepdims=True), sk)
            p = jnp.exp(s - m)
            denom = jnp.sum(p, axis=-1, keepdims=True) + jnp.exp(sk - m)
            o = jnp.dot(p.astype(BF16), v3, preferred_element_type=F32) / denom
            o_ref[0, :, hd * HEAD_DIM_A:(hd + 1) * HEAD_DIM_A] = o.astype(BF16)

    for h in range(N_HEADS_R):
        sl = slice(h * HEAD_DIM_R, (h + 1) * HEAD_DIM_R)
        qh = rq_ref[0, :, sl]
        kh_ = rk_ref[0, :, sl]
        vh = rv_ref[0, :, sl]
        qk = lax.dot_general(qh, kh_, (((1,), (1,)), ((), ())), preferred_element_type=F32)
        qf = qh.astype(F32)
        acc = None
        for d, (st_ref, g_ref) in enumerate(((sf_ref, gf_ref), (sb_ref, gb_ref))):
            y = jnp.dot((qk * dec_ref[d, h]).astype(BF16), vh, preferred_element_type=F32)
            y = y + jnp.dot((qf * xi_ref[d, h]).astype(BF16), st_ref[0, 0, h],
                            preferred_element_type=F32)
            mu = jnp.mean(y, axis=-1, keepdims=True)
            yc = y - mu
            var = jnp.mean(yc * yc, axis=-1, keepdims=True)
            gate = g_ref[0, :, sl].astype(F32)
            term = gate * jax.nn.sigmoid(gate) * (yc * lax.rsqrt(var + EPS))
            acc = term if acc is None else acc + term
        o_ref[0, :, WA_Q + h * HEAD_DIM_R:WA_Q + (h + 1) * HEAD_DIM_R] = acc.astype(BF16)


def _mixer(lg2, sink, q, k, v, rq, rk, rv, gf, gb, sf, sb, bias):
    B, Lp, _ = q.shape
    nb = Lp // BLOCK
    cur = lambda n_: pl.BlockSpec((1, BLOCK, n_), lambda b, n: (b, n, 0))
    prev = pl.BlockSpec((1, BLOCK, WA_KV), lambda b, n: (b, jnp.maximum(n - 1, 0), 0))
    nxt = pl.BlockSpec((1, BLOCK, WA_KV), lambda b, n: (b, jnp.minimum(n + 1, nb - 1), 0))
    st = pl.BlockSpec((1, 1, N_HEADS_R, HEAD_DIM_R, HEAD_DIM_R), lambda b, n: (b, n, 0, 0, 0))
    smem = pl.BlockSpec(memory_space=pltpu.SMEM)
    tab = pltpu.VMEM((2, N_HEADS_R, BLOCK, HEAD_DIM_R), F32)
    return pl.pallas_call(
        functools.partial(_mixer_kernel, lp=Lp),
        grid=(B, nb),
        in_specs=[smem, smem, cur(WA_Q), prev, cur(WA_KV), nxt, prev, cur(WA_KV), nxt,
                  cur(WR), cur(WR), cur(WR), cur(WR), cur(WR), st, st,
                  pl.BlockSpec((N_HEADS_A, BLOCK, 3 * BLOCK), lambda b, n: (0, 0, 0))],
        out_specs=pl.BlockSpec((1, BLOCK, D_MODEL), lambda b, n: (b, n, 0)),
        out_shape=jax.ShapeDtypeStruct((B, Lp, D_MODEL), BF16),
        scratch_shapes=[tab, tab],
        compiler_params=_params(("arbitrary", "arbitrary"), 32 << 20),
        name="mixer",
    )(lg2, sink, q, k, k, k, v, v, v, rq, rk, rv, gf, gb, sf, sb, bias)


def _out_proj_kernel(mix_ref, h_ref, w_ref, g2_ref, wr_hi_ref, wr_lo_ref,
                     h1_ref, hn_ref, aff_ref, *, tm):
    j = pl.program_id(1)
    h1 = h_ref[0] + jnp.dot(mix_ref[0], w_ref[...], preferred_element_type=F32)
    h1_ref[0] = h1
    ms = jnp.mean(h1 * h1, axis=-1, keepdims=True)
    xn = h1 * lax.rsqrt(ms + EPS) * g2_ref[...]
    hn_ref[0] = xn.astype(BF16)
    hi = xn.astype(BF16)
    lo = (xn - hi.astype(F32)).astype(BF16)
    logits = (jnp.dot(hi, wr_hi_ref[...], preferred_element_type=F32)
              + jnp.dot(hi, wr_lo_ref[...], preferred_element_type=F32)
              + jnp.dot(lo, wr_hi_ref[...], preferred_element_type=F32))
    lt = logits.T[:N_EXPERTS]
    mx = jnp.max(lt, axis=0, keepdims=True)
    ex = jnp.exp(lt - mx)
    aff = ex / jnp.sum(ex, axis=0, keepdims=True)
    col = j * tm + lax.broadcasted_iota(I32, (1, tm), 1)
    aff_ref[0] = jnp.where(col >= PAD_FRONT, aff, -1.0)


def _out_proj(mixed, hp, w_out_bf, g2, wr_hi, wr_lo):
    B, Lp, D = hp.shape
    tm = _row_tile(Lp)
    const = lambda b, j: (0, 0)
    row = lambda: pl.BlockSpec((1, tm, D), lambda b, j: (b, j, 0))
    return pl.pallas_call(
        functools.partial(_out_proj_kernel, tm=tm),
        grid=(B, Lp // tm),
        in_specs=[row(), row(), pl.BlockSpec((D, D), const), pl.BlockSpec((1, D), const),
                  pl.BlockSpec((D, LANES), const), pl.BlockSpec((D, LANES), const)],
        out_specs=[row(), row(), pl.BlockSpec((1, N_EXPERTS, tm), lambda b, j: (b, 0, j))],
        out_shape=[jax.ShapeDtypeStruct((B, Lp, D), F32),
                   jax.ShapeDtypeStruct((B, Lp, D), BF16),
                   jax.ShapeDtypeStruct((B, N_EXPERTS, Lp), F32)],
        compiler_params=_params(("parallel", "parallel"), 32 << 20),
        name="out_proj",
    )(mixed, hp, w_out_bf, g2, wr_hi, wr_lo)


def _route_kernel(aff_ref, posm_ref, cpos_ref, *, cap, nb):
    E = N_EXPERTS

    def keys(sl=slice(None)):
        return pltpu.bitcast(aff_ref[0, :, sl], I32)

    def count_ge(t):
        return jnp.sum((keys() >= t).astype(I32), axis=-1, keepdims=True)

    def search(_, carry):
        lo, hi = carry
        mid = lo + (hi - lo) // 2
        ok = count_ge(mid) >= cap
        return jnp.where(ok, mid, lo), jnp.where(ok, hi, mid)

    lo0 = jnp.zeros((E, 1), I32)
    hi0 = jnp.full((E, 1), 0x3F800001, I32)
    thr, _ = lax.fori_loop(0, 31, search, (lo0, hi0))
    n_gt = jnp.sum((keys() > thr).astype(F32), axis=-1, keepdims=True)
    need_eq = float(cap) - n_gt

    jr = lax.broadcasted_iota(I32, (LANES, LANES), 0)
    jc = lax.broadcasted_iota(I32, (LANES, LANES), 1)
    upper = (jr < jc).astype(F32).astype(BF16)
    lane = lax.broadcasted_iota(I32, (E, LANES), 1)

    def chunk(c, carry):
        cg, ce, cp = carry
        off = pl.multiple_of(c * LANES, LANES)
        kk = pltpu.bitcast(aff_ref[0, :, pl.ds(off, LANES)], I32)
        gt = kk > thr
        eq = kk == thr
        both = jnp.concatenate([gt, eq], axis=0).astype(F32)
        ex = jnp.dot(both.astype(BF16), upper, preferred_element_type=F32)
        exg = ex[:E] + cg
        exe = ex[E:] + ce
        sel = gt | (eq & (exe < need_eq))
        pos = exg + jnp.minimum(exe, need_eq)
        posm_ref[0, :, pl.ds(off, LANES)] = jnp.where(sel, pos, -1.0).astype(I32)
        cp = jnp.where(lane == c, (cg + jnp.minimum(ce, need_eq)).astype(I32), cp)
        tot = jnp.sum(both, axis=-1, keepdims=True)
        return cg + tot[:E], ce + tot[E:], cp

    zero = jnp.zeros((E, 1), F32)
    cg, ce, cp = lax.fori_loop(0, nb, chunk, (zero, zero, jnp.zeros((E, LANES), I32)))
    cpos_ref[0] = jnp.where(lane == nb, (cg + jnp.minimum(ce, need_eq)).astype(I32), cp)


def _route(aff_t, cap):
    B, E, Lp = aff_t.shape
    nb = Lp // LANES
    assert nb + 1 <= LANES
    return pl.pallas_call(
        functools.partial(_route_kernel, cap=cap, nb=nb),
        grid=(B,),
        in_specs=[pl.BlockSpec((1, E, Lp), lambda b: (b, 0, 0))],
        out_specs=[pl.BlockSpec((1, E, Lp), lambda b: (b, 0, 0)),
                   pl.BlockSpec((1, E, LANES), lambda b: (b, 0, 0))],
        out_shape=[jax.ShapeDtypeStruct((B, E, Lp), I32),
                   jax.ShapeDtypeStruct((B, E, LANES), I32)],
        compiler_params=_params(("parallel",), 16 << 20),
        name="route",
    )(aff_t)


NARROW_WINDOW = 3 * BF16_SUBLANES
WIDE_WINDOW = BLOCK + BF16_SUBLANES


def _window(cpos_ref, b, e, c, w, cpad):
    base = (b * N_EXPERTS + e) * LANES + c
    a0 = jnp.minimum((cpos_ref[base] // BF16_SUBLANES) * BF16_SUBLANES, cpad - w)
    return pl.multiple_of(a0, BF16_SUBLANES), cpos_ref[base + 1] - a0 <= w


def _all_fit(cpos_ref, b, experts, c, cpad):
    fits = None
    for e in experts:
        _, ok = _window(cpos_ref, b, e, c, NARROW_WINDOW, cpad)
        fits = ok if fits is None else fits & ok
    return fits


EXPERT_GROUP = 8


def _gather_kernel(cpos_ref, hn_ref, posm_ref, x_ref, *, nsub, cpad):
    b = pl.program_id(0)
    g = pl.program_id(1)
    co = pl.program_id(2)
    experts = [g * EXPERT_GROUP + e for e in range(EXPERT_GROUP)]

    @pl.when(co == 0)
    def _():
        x_ref[...] = jnp.zeros_like(x_ref)

    def sub_chunk(i, carry):
        c = co * nsub + i
        off = pl.multiple_of(i * BLOCK, BLOCK)

        def move(w):
            rows = lax.broadcasted_iota(I32, (w, BLOCK), 0)
            starts = []
            onehots = []
            for e in range(EXPERT_GROUP):
                a0, _ = _window(cpos_ref, b, experts[e], c, w, cpad)
                starts.append(a0)
                hit = rows + a0 == posm_ref[0, 0, e:e + 1, pl.ds(off, BLOCK)]
                onehots.append(hit.astype(F32).astype(BF16))
            res = jnp.dot(jnp.concatenate(onehots, axis=0), hn_ref[0, pl.ds(off, BLOCK), :],
                          preferred_element_type=F32)
            for e in range(EXPERT_GROUP):
                win = x_ref[0, e, pl.ds(starts[e], w), :].astype(F32) + res[e * w:(e + 1) * w]
                x_ref[0, e, pl.ds(starts[e], w), :] = win.astype(BF16)

        narrow = _all_fit(cpos_ref, b, experts, c, cpad)
        pl.when(narrow)(functools.partial(move, NARROW_WINDOW))
        pl.when(jnp.logical_not(narrow))(functools.partial(move, WIDE_WINDOW))
        return carry

    lax.fori_loop(0, nsub, sub_chunk, 0)


def _gather(cpos_flat, hn, posm, cpad):
    B, Lp, D = hn.shape
    E = N_EXPERTS
    ng = E // EXPERT_GROUP
    tm = _row_tile(Lp)
    posm4 = posm.reshape(B, ng, EXPERT_GROUP, Lp)
    grid_spec = pltpu.PrefetchScalarGridSpec(
        num_scalar_prefetch=1,
        grid=(B, ng, Lp // tm),
        in_specs=[pl.BlockSpec((1, tm, D), lambda b, g, c, cp: (b, c, 0)),
                  pl.BlockSpec((1, 1, EXPERT_GROUP, tm), lambda b, g, c, cp: (b, g, 0, c))],
        out_specs=pl.BlockSpec((1, EXPERT_GROUP, cpad, D), lambda b, g, c, cp: (b, g, 0, 0)),
    )
    return pl.pallas_call(
        functools.partial(_gather_kernel, nsub=tm // BLOCK, cpad=cpad),
        grid_spec=grid_spec,
        out_shape=jax.ShapeDtypeStruct((B, E, cpad, D), BF16),
        compiler_params=_params(("arbitrary", "arbitrary", "arbitrary"), 48 << 20),
        name="gather",
    )(cpos_flat, hn, posm4)


FFN_TILE = 256
FFN_BATCH = 2


def _ffn_kernel(x_ref, wg_ref, wu_ref, wd_ref, y_ref, hm_ref, wd_bf_ref, *, tf):
    f = pl.program_id(2)
    bh, _, cpad, _ = x_ref.shape
    x = x_ref[...].reshape(bh * cpad, D_MODEL)
    a = jnp.dot(x, wg_ref[0, 0].astype(BF16), preferred_element_type=F32)
    u = jnp.dot(x, wu_ref[0, 0].astype(BF16), preferred_element_type=F32)
    off = pl.multiple_of(f * tf, tf)
    hm_ref[:, pl.ds(off, tf)] = (a * jax.nn.sigmoid(a) * u).astype(BF16)
    wd_bf_ref[pl.ds(off, tf), :] = wd_ref[0, 0].astype(BF16)

    @pl.when(f == pl.num_programs(2) - 1)
    def _():
        for i in range(bh):
            y_ref[i, 0] = jnp.dot(hm_ref[i * cpad:(i + 1) * cpad, :], wd_bf_ref[...],
                                  preferred_element_type=F32).astype(BF16)


def _ffn(xe, w_gate, w_up, w_down, layer):
    B, E, cpad, D = xe.shape
    F = w_gate.shape[-1]
    tf = FFN_TILE if F % FFN_TILE == 0 else LANES
    bh = FFN_BATCH if B % FFN_BATCH == 0 else 1
    xspec = pl.BlockSpec((bh, 1, cpad, D), lambda e, b, f: (b, e, 0, 0))
    return pl.pallas_call(
        functools.partial(_ffn_kernel, tf=tf),
        grid=(E, B // bh, F // tf),
        in_specs=[xspec,
                  pl.BlockSpec((1, 1, D, tf), lambda e, b, f: (layer, e, 0, f)),
                  pl.BlockSpec((1, 1, D, tf), lambda e, b, f: (layer, e, 0, f)),
                  pl.BlockSpec((1, 1, tf, D), lambda e, b, f: (layer, e, f, 0))],
        out_specs=xspec,
        out_shape=jax.ShapeDtypeStruct(xe.shape, BF16),
        scratch_shapes=[pltpu.VMEM((bh * cpad, F), BF16), pltpu.VMEM((F, D), BF16)],
        compiler_params=_params(("parallel", "parallel", "arbitrary"), 56 << 20),
        name="ffn",
    )(xe, w_gate, w_up, w_down)


def _combine_kernel(cpos_ref, h_ref, posm_ref, gate_ref, ye_ref, o_ref, *, nsub, cpad):
    b = pl.program_id(0)
    co = pl.program_id(1)
    experts = list(range(N_EXPERTS))

    def sub_chunk(i, carry):
        c = co * nsub + i
        off = pl.multiple_of(i * BLOCK, BLOCK)

        def move(w):
            rows = lax.broadcasted_iota(I32, (w, BLOCK), 0)
            weights = []
            windows = []
            for e in experts:
                a0, _ = _window(cpos_ref, b, e, c, w, cpad)
                hit = rows + a0 == posm_ref[0, e:e + 1, pl.ds(off, BLOCK)]
                weights.append(jnp.where(hit, gate_ref[0, e:e + 1, pl.ds(off, BLOCK)], 0.0).astype(BF16))
                windows.append(ye_ref[0, e, pl.ds(a0, w), :])
            moe = lax.dot_general(jnp.concatenate(weights, axis=0), jnp.concatenate(windows, axis=0),
                                  (((0,), (0,)), ((), ())), preferred_element_type=F32)
            o_ref[0, pl.ds(off, BLOCK), :] = h_ref[0, pl.ds(off, BLOCK), :] + moe

        narrow = _all_fit(cpos_ref, b, experts, c, cpad)
        pl.when(narrow)(functools.partial(move, NARROW_WINDOW))
        pl.when(jnp.logical_not(narrow))(functools.partial(move, WIDE_WINDOW))
        return carry

    lax.fori_loop(0, nsub, sub_chunk, 0)


def _combine(cpos_flat, h1, posm, gate, ye, cpad):
    B, Lp, D = h1.shape
    E = N_EXPERTS
    tm = _row_tile(Lp)
    grid_spec = pltpu.PrefetchScalarGridSpec(
        num_scalar_prefetch=1,
        grid=(B, Lp // tm),
        in_specs=[pl.BlockSpec((1, tm, D), lambda b, c, cp: (b, c, 0)),
                  pl.BlockSpec((1, E, tm), lambda b, c, cp: (b, 0, c)),
                  pl.BlockSpec((1, E, tm), lambda b, c, cp: (b, 0, c)),
                  pl.BlockSpec((1, E, cpad, D), lambda b, c, cp: (b, 0, 0, 0),
                               pipeline_mode=pl.Buffered(1))],
        out_specs=pl.BlockSpec((1, tm, D), lambda b, c, cp: (b, c, 0)),
    )
    return pl.pallas_call(
        functools.partial(_combine_kernel, nsub=tm // BLOCK, cpad=cpad),
        grid_spec=grid_spec,
        out_shape=jax.ShapeDtypeStruct((B, Lp, D), F32),
        compiler_params=_params(("arbitrary", "arbitrary"), VMEM_LIMIT_CAP),
        name="combine",
    )(cpos_flat, h1, posm, gate, ye)


def _t5_bucket(rel):
    half = N_BUCKETS // 2
    max_exact = half // 2
    n = jnp.abs(rel)
    large = max_exact + (jnp.log(jnp.maximum(n, max_exact).astype(F32) / max_exact)
                         / math.log(MAX_DISTANCE / max_exact) * (half - max_exact)).astype(I32)
    large = jnp.minimum(large, half - 1)
    return jnp.where(rel > 0, half, 0) + jnp.where(n < max_exact, n, large)


def _block_diag_mean(n, group):
    i = jnp.arange(n)
    return jnp.where((i[:, None] // group) == (i[None, :] // group), 1.0 / group, 0.0).astype(BF16)


def kernel(x, meta_tokens, rel_bias, norm1_g, w_in, q_norm_g, k_norm_g, attn_sink, ret_decay,
           w_out, norm2_g, w_router, w_gate, w_up, w_down):
    B, seq, D = x.shape
    depth = w_in.shape[0]
    L = seq + N_META
    Lp = L + PAD_FRONT
    assert D == D_MODEL and Lp % BLOCK == 0
    cap = CAPACITY_FACTOR * L // N_EXPERTS
    cpad = -(-max(cap, WIDE_WINDOW) // BF16_SUBLANES) * BF16_SUBLANES

    hp = jnp.concatenate([jnp.zeros((B, PAD_FRONT, D), x.dtype),
                          jnp.broadcast_to(meta_tokens.astype(x.dtype)[None], (B, N_META, D)), x], axis=1)

    pos = jnp.arange(Lp, dtype=F32)
    inv = ROPE_BASE ** (-jnp.arange(0, HEAD_DIM_R, 2, dtype=F32) / HEAD_DIM_R)
    ang = pos[:, None] * inv[None]
    cc = jnp.concatenate([jnp.cos(ang), jnp.cos(ang)], axis=-1)
    ss = jnp.concatenate([-jnp.sin(ang), jnp.sin(ang)], axis=-1)
    rel = (jnp.arange(3 * BLOCK)[None, :] - BLOCK) - jnp.arange(BLOCK)[:, None]
    bucket_onehot = (_t5_bucket(rel)[None] == jnp.arange(N_BUCKETS)[:, None, None]).astype(F32)
    bias = jnp.einsum('kqs,kh->hqs', bucket_onehot, rel_bias.astype(F32), precision=lax.Precision.HIGHEST)
    bias = jnp.where((jnp.abs(rel) <= BLOCK)[None], bias, NEG)
    bdq = _block_diag_mean(WA_Q, HEAD_DIM_A)
    bdk = _block_diag_mean(WA_KV, HEAD_DIM_A)

    for l in range(depth):
        qg = (jnp.tile(q_norm_g[l].astype(F32), N_HEADS_A) * HEAD_DIM_A ** -0.5)[None]
        kg = jnp.tile(k_norm_g[l].astype(F32), N_KV_A)[None]
        lg2 = -jnp.exp(ret_decay[l].astype(F32))
        q, k, v, rq, rk, rv, gf, gb = _in_proj(hp, norm1_g[l][None].astype(F32), w_in[l].astype(BF16),
                                               cc, ss, qg, kg, bdq, bdk)
        sf, sb = _ret_states(lg2, rk, rv)
        mixed = _mixer(lg2, attn_sink[l].astype(F32), q, k, v, rq, rk, rv, gf, gb, sf, sb, bias)
        wr = jnp.pad(w_router[l].astype(F32), ((0, 0), (0, LANES - N_EXPERTS)))
        wr_hi = wr.astype(BF16)
        wr_lo = (wr - wr_hi.astype(F32)).astype(BF16)
        h1, hn, aff = _out_proj(mixed, hp, w_out[l].astype(BF16), norm2_g[l][None].astype(F32), wr_hi, wr_lo)
        posm, cpos = _route(aff, cap)
        cpos_flat = cpos.reshape(-1)
        xe = _gather(cpos_flat, hn, posm, cpad)
        ye = _ffn(xe, w_gate, w_up, w_down, l)
        hp = _combine(cpos_flat, h1, posm, aff, ye, cpad)
    return hp[:, PAD_FRONT + N_META:]
```

```python
import functools
import math

import jax
import jax.numpy as jnp
from jax import lax
from jax.experimental import pallas as pl
from jax.experimental.pallas import tpu as pltpu

D_MODEL = 1024
N_META = 16
BLOCK = 128
PAD_FRONT = BLOCK - N_META
EPS = 1e-6
N_HEADS_A = 8
N_KV_A = 2
GROUP_A = N_HEADS_A // N_KV_A
HEAD_DIM_A = 64
N_HEADS_R = 4
HEAD_DIM_R = 128
ROPE_BASE = 10000.0
N_BUCKETS = 32
MAX_DISTANCE = 128
N_EXPERTS = 16
CAPACITY_FACTOR = 2
WA_Q = N_HEADS_A * HEAD_DIM_A
WA_KV = N_KV_A * HEAD_DIM_A
WR = N_HEADS_R * HEAD_DIM_R
D_IN = WA_Q + 2 * WA_KV + 5 * WR
NEG = -1e30

LANES = 128
BF16_SUBLANES = 16
VMEM_LIMIT_CAP = 60000 * 1024

F32 = jnp.float32
BF16 = jnp.bfloat16
I32 = jnp.int32


def _params(semantics, vmem_bytes):
    return pltpu.CompilerParams(dimension_semantics=semantics,
                                vmem_limit_bytes=min(int(vmem_bytes), VMEM_LIMIT_CAP))


def _row_tile(lp):
    best = BLOCK
    for t in range(BLOCK, 1024 + 1, BLOCK):
        if lp % t == 0:
            best = t
    return best


def _split_dot(x, w_ref):
    hi = x.astype(BF16)
    lo = (x - hi.astype(F32)).astype(BF16)
    w = w_ref[...]
    return (jnp.dot(hi, w, preferred_element_type=F32) + jnp.dot(lo, w, preferred_element_type=F32))


def _in_proj_kernel(h_ref, g1_ref, w_ref, cc_ref, ss_ref, qg_ref, kg_ref, bdq_ref, bdk_ref,
                    q_ref, k_ref, v_ref, rq_ref, rk_ref, rv_ref, gf_ref, gb_ref, *, tm):
    j = pl.program_id(1)
    x = h_ref[0]
    ms = jnp.mean(x * x, axis=-1, keepdims=True)
    xn = x * lax.rsqrt(ms + EPS) * g1_ref[...]
    row = j * tm + lax.broadcasted_iota(I32, (tm, 1), 0)
    xn = jnp.where(row >= PAD_FRONT, xn, 0.0).astype(BF16)

    def proj(lo, n):
        return jnp.dot(xn, w_ref[:, lo:lo + n], preferred_element_type=F32)

    q = proj(0, WA_Q)
    q_ref[0] = (q * lax.rsqrt(_split_dot(q * q, bdq_ref) + EPS) * qg_ref[...]).astype(BF16)
    k = proj(WA_Q, WA_KV)
    k = k * lax.rsqrt(_split_dot(k * k, bdk_ref) + EPS) * kg_ref[...]
    v = proj(WA_Q + WA_KV, WA_KV)
    lower = lax.broadcasted_iota(I32, (1, WA_KV), 1) < HEAD_DIM_A
    for t, ref in ((k, k_ref), (v, v_ref)):
        head0_lo = jnp.where(lower, t, 0.0)
        head1_hi = jnp.where(lower, 0.0, t)
        variants = (head0_lo, pltpu.roll(head0_lo, HEAD_DIM_A, 1), pltpu.roll(head1_hi, HEAD_DIM_A, 1), head1_hi)
        for i, var in enumerate(variants):
            ref[0, :, i * WA_KV:(i + 1) * WA_KV] = var.astype(BF16)

    cc = cc_ref[...]
    ss = ss_ref[...]
    base = WA_Q + 2 * WA_KV
    for off, ref, scale in ((base, rq_ref, 1.0), (base + WR, rk_ref, HEAD_DIM_R ** -0.5)):
        r = proj(off, WR)
        for hh in range(N_HEADS_R):
            seg = r[:, hh * HEAD_DIM_R:(hh + 1) * HEAD_DIM_R]
            rot = seg * cc + pltpu.roll(seg, HEAD_DIM_R // 2, 1) * ss
            ref[0, :, hh * HEAD_DIM_R:(hh + 1) * HEAD_DIM_R] = (rot * scale).astype(BF16)
    rv_ref[0] = proj(base + 2 * WR, WR).astype(BF16)
    gf_ref[0] = proj(base + 3 * WR, WR).astype(BF16)
    gb_ref[0] = proj(base + 4 * WR, WR).astype(BF16)


def _in_proj(hp, g1, w_in_bf, cc, ss, qg, kg, bdq, bdk):
    B, Lp, D = hp.shape
    tm = _row_tile(Lp)
    const = lambda b, j: (0, 0)
    row = lambda n: pl.BlockSpec((1, tm, n), lambda b, j: (b, j, 0))
    widths = (WA_Q, 2 * N_KV_A * WA_KV, 2 * N_KV_A * WA_KV, WR, WR, WR, WR, WR)
    return pl.pallas_call(
        functools.partial(_in_proj_kernel, tm=tm),
        grid=(B, Lp // tm),
        in_specs=[row(D),
                  pl.BlockSpec((1, D), const),
                  pl.BlockSpec((D, D_IN), const),
                  pl.BlockSpec((tm, HEAD_DIM_R), lambda b, j: (j, 0)),
                  pl.BlockSpec((tm, HEAD_DIM_R), lambda b, j: (j, 0)),
                  pl.BlockSpec((1, WA_Q), const),
                  pl.BlockSpec((1, WA_KV), const),
                  pl.BlockSpec((WA_Q, WA_Q), const),
                  pl.BlockSpec((WA_KV, WA_KV), const)],
        out_specs=[row(n) for n in widths],
        out_shape=[jax.ShapeDtypeStruct((B, Lp, n), BF16) for n in widths],
        compiler_params=_params(("parallel", "parallel"), 48 << 20),
        name="in_proj",
    )(hp, g1, w_in_bf, cc, ss, qg, kg, bdq, bdk)


def _ret_state_kernel(lg_ref, kf_ref, vf_ref, kb_ref, vb_ref, sf_ref, sb_ref, st_ref):
    n = pl.program_id(0)

    @pl.when(n == 0)
    def _():
        st_ref[...] = jnp.zeros_like(st_ref)

    idx = lax.broadcasted_iota(I32, (BLOCK, 1), 0).astype(F32)
    ones_row = jnp.ones((1, HEAD_DIM_R), F32)
    for d, (k_ref, v_ref, s_out) in enumerate(((kf_ref, vf_ref, sf_ref), (kb_ref, vb_ref, sb_ref))):
        for h in range(N_HEADS_R):
            lg = lg_ref[d, h]
            sl = slice(h * HEAD_DIM_R, (h + 1) * HEAD_DIM_R)
            zeta = jnp.exp(lg * ((BLOCK - 1.0 - idx) if d == 0 else idx))
            chunk_decay = jnp.exp(lg * float(BLOCK) * ones_row)
            for b in range(st_ref.shape[0]):
                st = st_ref[b, d, h]
                s_out[b, 0, h] = st.astype(BF16)
                kz = (k_ref[b, :, sl].astype(F32) * zeta).astype(BF16)
                kv = lax.dot_general(kz, v_ref[b, :, sl], (((0,), (0,)), ((), ())),
                                     preferred_element_type=F32)
                st_ref[b, d, h] = st * chunk_decay + kv


def _ret_states(lg2, rk, rv):
    B, Lp, _ = rk.shape
    nb = Lp // BLOCK
    fwd = pl.BlockSpec((B, BLOCK, WR), lambda n: (0, n, 0))
    bwd = pl.BlockSpec((B, BLOCK, WR), lambda n: (0, nb - 1 - n, 0))
    st_shape = (B, 1, N_HEADS_R, HEAD_DIM_R, HEAD_DIM_R)
    return pl.pallas_call(
        _ret_state_kernel,
        grid=(nb,),
        in_specs=[pl.BlockSpec(memory_space=pltpu.SMEM), fwd, fwd, bwd, bwd],
        out_specs=[pl.BlockSpec(st_shape, lambda n: (0, n, 0, 0, 0)),
                   pl.BlockSpec(st_shape, lambda n: (0, nb - 1 - n, 0, 0, 0))],
        out_shape=[jax.ShapeDtypeStruct((B, nb) + st_shape[2:], BF16)] * 2,
        scratch_shapes=[pltpu.VMEM((B, 2, N_HEADS_R, HEAD_DIM_R, HEAD_DIM_R), F32)],
        compiler_params=_params(("arbitrary",), 24 << 20),
        name="ret_state",
    )(lg2, rk, rv, rk, rv)


def _mixer_kernel(lg_ref, sink_ref, q_ref, kp_ref, kc_ref, kn_ref, vp_ref, vc_ref, vn_ref,
                  rq_ref, rk_ref, rv_ref, gf_ref, gb_ref, sf_ref, sb_ref, bias_ref,
                  o_ref, dec_ref, xi_ref, kbuf_ref, vbuf_ref, *, lp, tm):
    b = pl.program_id(0)
    c = pl.program_id(1)
    nsub = tm // BLOCK
    kv_w = 3 * BLOCK

    @pl.when((b == 0) & (c == 0))
    def _():
        ii = lax.broadcasted_iota(I32, (BLOCK, BLOCK), 0).astype(F32)
        jj = lax.broadcasted_iota(I32, (BLOCK, BLOCK), 1).astype(F32)
        for d in range(2):
            diff = (ii - jj) if d == 0 else (jj - ii)
            reach = (ii + 1.0) if d == 0 else (float(BLOCK) - ii)
            for h in range(N_HEADS_R):
                lg = lg_ref[d, h]
                dec_ref[d, h] = jnp.where(diff >= 0, jnp.exp(lg * jnp.maximum(diff, 0.0)), 0.0)
                xi_ref[d, h] = jnp.exp(lg * reach)

    for buf, (p_ref, c_ref, n_ref) in ((kbuf_ref, (kp_ref, kc_ref, kn_ref)), (vbuf_ref, (vp_ref, vc_ref, vn_ref))):
        buf[0:BLOCK] = p_ref[0]
        buf[BLOCK:BLOCK + tm] = c_ref[0]
        buf[BLOCK + tm:] = n_ref[0]

    lower_lanes = lax.broadcasted_iota(I32, (1, 2 * HEAD_DIM_A), 1) < HEAD_DIM_A
    upper_rows = lax.broadcasted_iota(I32, (2 * BLOCK, 1), 0) >= BLOCK

    def block(i, carry):
        r0 = pl.multiple_of(i * BLOCK, BLOCK)
        rows = pl.ds(r0, BLOCK)
        win = pl.ds(r0, kv_w)

        kpos = (c * nsub + i - 1) * BLOCK + lax.broadcasted_iota(I32, (1, kv_w), 1)
        kvalid = (kpos >= PAD_FRONT) & (kpos < lp)
        kvalid = jnp.concatenate([kvalid, kvalid], axis=1)
        for kh in range(N_KV_A):
            g0 = slice(2 * kh * WA_KV, (2 * kh + 1) * WA_KV)
            g1 = slice((2 * kh + 1) * WA_KV, (2 * kh + 2) * WA_KV)
            q2 = jnp.concatenate([q_ref[0, rows, g0], q_ref[0, rows, g1]], axis=0)
            kcat = jnp.concatenate([kbuf_ref[win, g0], kbuf_ref[win, g1]], axis=0)
            vcat = jnp.concatenate([vbuf_ref[win, g0], vbuf_ref[win, g1]], axis=0)
            s = lax.dot_general(q2, kcat, (((1,), (1,)), ((), ())), preferred_element_type=F32)
            s = jnp.where(kvalid, s + bias_ref[kh], NEG)
            probs = []
            denoms = []
            for half in range(2):
                sh = s[:, half * kv_w:(half + 1) * kv_w]
                sk = jnp.where(upper_rows, sink_ref[4 * kh + 2 + half], sink_ref[4 * kh + half])
                m = jnp.maximum(jnp.max(sh, axis=-1, keepdims=True), sk)
                p = jnp.exp(sh - m)
                denoms.append(jnp.sum(p, axis=-1, keepdims=True) + jnp.exp(sk - m))
                probs.append(p.astype(BF16))
            o = jnp.dot(jnp.concatenate(probs, axis=1), vcat, preferred_element_type=F32)
            o = o / jnp.where(lower_lanes, denoms[0], denoms[1])
            o_ref[0, rows, g0] = o[:BLOCK].astype(BF16)
            o_ref[0, rows, g1] = o[BLOCK:].astype(BF16)

        for h in range(N_HEADS_R):
            sl = slice(h * HEAD_DIM_R, (h + 1) * HEAD_DIM_R)
            qh = rq_ref[0, rows, sl]
            vh = rv_ref[0, rows, sl]
            qk = lax.dot_general(qh, rk_ref[0, rows, sl], (((1,), (1,)), ((), ())),
                                 preferred_element_type=F32)
            qf = qh.astype(F32)
            acc = None
            for d, (st_ref, g_ref) in enumerate(((sf_ref, gf_ref), (sb_ref, gb_ref))):
                lhs = jnp.concatenate([(qk * dec_ref[d, h]).astype(BF16), (qf * xi_ref[d, h]).astype(BF16)], axis=1)
                rhs = jnp.concatenate([vh, st_ref[0, i, h]], axis=0)
                y = jnp.dot(lhs, rhs, preferred_element_type=F32)
                mu = jnp.mean(y, axis=-1, keepdims=True)
                yc = y - mu
                var = jnp.mean(yc * yc, axis=-1, keepdims=True)
                gate = g_ref[0, rows, sl].astype(F32)
                term = gate * jax.nn.sigmoid(gate) * (yc * lax.rsqrt(var + EPS))
                acc = term if acc is None else acc + term
            o_ref[0, rows, WA_Q + h * HEAD_DIM_R:WA_Q + (h + 1) * HEAD_DIM_R] = acc.astype(BF16)
        return carry

    lax.fori_loop(0, nsub, block, 0)


def _mixer(lg2, sink, q, k4, v4, rq, rk, rv, gf, gb, sf, sb, bias2):
    B, Lp, _ = q.shape
    nb = Lp // BLOCK
    tm = _row_tile(Lp)
    nsub = tm // BLOCK
    kvw = k4.shape[-1]
    cur = lambda n_: pl.BlockSpec((1, tm, n_), lambda b, c: (b, c, 0))
    prev = pl.BlockSpec((1, BLOCK, kvw), lambda b, c: (b, jnp.maximum(c * nsub - 1, 0), 0))
    nxt = pl.BlockSpec((1, BLOCK, kvw), lambda b, c: (b, jnp.minimum((c + 1) * nsub, nb - 1), 0))
    st = pl.BlockSpec((1, nsub, N_HEADS_R, HEAD_DIM_R, HEAD_DIM_R), lambda b, c: (b, c, 0, 0, 0))
    smem = pl.BlockSpec(memory_space=pltpu.SMEM)
    tab = pltpu.VMEM((2, N_HEADS_R, BLOCK, HEAD_DIM_R), F32)
    kvbuf = pltpu.VMEM((tm + 2 * BLOCK, kvw), BF16)
    return pl.pallas_call(
        functools.partial(_mixer_kernel, lp=Lp, tm=tm),
        grid=(B, Lp // tm),
        in_specs=[smem, smem, cur(WA_Q), prev, cur(kvw), nxt, prev, cur(kvw), nxt,
                  cur(WR), cur(WR), cur(WR), cur(WR), cur(WR), st, st,
                  pl.BlockSpec(bias2.shape, lambda b, c: (0, 0, 0))],
        out_specs=pl.BlockSpec((1, tm, D_MODEL), lambda b, c: (b, c, 0)),
        out_shape=jax.ShapeDtypeStruct((B, Lp, D_MODEL), BF16),
        scratch_shapes=[tab, tab, kvbuf, kvbuf],
        compiler_params=_params(("arbitrary", "arbitrary"), 40 << 20),
        name="mixer",
    )(lg2, sink, q, k4, k4, k4, v4, v4, v4, rq, rk, rv, gf, gb, sf, sb, bias2)


def _out_proj_kernel(mix_ref, h_ref, w_ref, g2_ref, wr_ref, h1_ref, hn_ref, aff_ref, *, tm):
    j = pl.program_id(1)
    h1 = h_ref[0] + jnp.dot(mix_ref[0], w_ref[...], preferred_element_type=F32)
    h1_ref[0] = h1
    ms = jnp.mean(h1 * h1, axis=-1, keepdims=True)
    xn = h1 * lax.rsqrt(ms + EPS) * g2_ref[...]
    hn_ref[0] = xn.astype(BF16)
    hi = xn.astype(BF16)
    lo = (xn - hi.astype(F32)).astype(BF16)
    both = jnp.dot(hi, wr_ref[...], preferred_element_type=F32)
    logits = (both[:, :LANES] + both[:, LANES:]
              + jnp.dot(lo, wr_ref[:, :LANES], preferred_element_type=F32))
    lt = logits.T[:N_EXPERTS]
    mx = jnp.max(lt, axis=0, keepdims=True)
    ex = jnp.exp(lt - mx)
    aff = ex / jnp.sum(ex, axis=0, keepdims=True)
    col = j * tm + lax.broadcasted_iota(I32, (1, tm), 1)
    aff_ref[0] = jnp.where(col >= PAD_FRONT, aff, -1.0)


def _out_proj(mixed, hp, w_out_bf, g2, wr_split):
    B, Lp, D = hp.shape
    tm = _row_tile(Lp)
    const = lambda b, j: (0, 0)
    row = lambda: pl.BlockSpec((1, tm, D), lambda b, j: (b, j, 0))
    return pl.pallas_call(
        functools.partial(_out_proj_kernel, tm=tm),
        grid=(B, Lp // tm),
        in_specs=[row(), row(), pl.BlockSpec((D, D), const), pl.BlockSpec((1, D), const),
                  pl.BlockSpec((D, 2 * LANES), const)],
        out_specs=[row(), row(), pl.BlockSpec((1, N_EXPERTS, tm), lambda b, j: (b, 0, j))],
        out_shape=[jax.ShapeDtypeStruct((B, Lp, D), F32),
                   jax.ShapeDtypeStruct((B, Lp, D), BF16),
                   jax.ShapeDtypeStruct((B, N_EXPERTS, Lp), F32)],
        compiler_params=_params(("parallel", "parallel"), 32 << 20),
        name="out_proj",
    )(mixed, hp, w_out_bf, g2, wr_split)


def _route_kernel(aff_ref, posm_ref, cpos_ref, *, cap, nb):
    E = N_EXPERTS

    def keys(sl=slice(None)):
        return pltpu.bitcast(aff_ref[0, :, sl], I32)

    def count_ge(t):
        return jnp.sum((keys() >= t).astype(I32), axis=-1, keepdims=True)

    def search(_, carry):
        lo, hi = carry
        mid = lo + (hi - lo) // 2
        ok = count_ge(mid) >= cap
        return jnp.where(ok, mid, lo), jnp.where(ok, hi, mid)

    lo0 = jnp.zeros((E, 1), I32)
    hi0 = jnp.full((E, 1), 0x3F800001, I32)
    thr, _ = lax.fori_loop(0, 31, search, (lo0, hi0))
    n_gt = jnp.sum((keys() > thr).astype(F32), axis=-1, keepdims=True)
    need_eq = float(cap) - n_gt

    jr = lax.broadcasted_iota(I32, (LANES, LANES), 0)
    jc = lax.broadcasted_iota(I32, (LANES, LANES), 1)
    upper = (jr < jc).astype(F32).astype(BF16)
    lane = lax.broadcasted_iota(I32, (E, LANES), 1)

    def chunk(c, carry):
        cg, ce, cp = carry
        off = pl.multiple_of(c * LANES, LANES)
        kk = pltpu.bitcast(aff_ref[0, :, pl.ds(off, LANES)], I32)
        gt = kk > thr
        eq = kk == thr
        both = jnp.concatenate([gt, eq], axis=0).astype(F32)
        ex = jnp.dot(both.astype(BF16), upper, preferred_element_type=F32)
        exg = ex[:E] + cg
        exe = ex[E:] + ce
        sel = gt | (eq & (exe < need_eq))
        pos = exg + jnp.minimum(exe, need_eq)
        posm_ref[0, :, pl.ds(off, LANES)] = jnp.where(sel, pos, -1.0).astype(I32)
        cp = jnp.where(lane == c, (cg + jnp.minimum(ce, need_eq)).astype(I32), cp)
        tot = jnp.sum(both, axis=-1, keepdims=True)
        return cg + tot[:E], ce + tot[E:], cp

    zero = jnp.zeros((E, 1), F32)
    cg, ce, cp = lax.fori_loop(0, nb, chunk, (zero, zero, jnp.zeros((E, LANES), I32)))
    cpos_ref[0] = jnp.where(lane == nb, (cg + jnp.minimum(ce, need_eq)).astype(I32), cp)


def _route(aff_t, cap):
    B, E, Lp = aff_t.shape
    nb = Lp // LANES
    assert nb + 1 <= LANES
    return pl.pallas_call(
        functools.partial(_route_kernel, cap=cap, nb=nb),
        grid=(B,),
        in_specs=[pl.BlockSpec((1, E, Lp), lambda b: (b, 0, 0))],
        out_specs=[pl.BlockSpec((1, E, Lp), lambda b: (b, 0, 0)),
                   pl.BlockSpec((1, E, LANES), lambda b: (b, 0, 0))],
        out_shape=[jax.ShapeDtypeStruct((B, E, Lp), I32),
                   jax.ShapeDtypeStruct((B, E, LANES), I32)],
        compiler_params=_params(("parallel",), 16 << 20),
        name="route",
    )(aff_t)


NARROW_WINDOW = 3 * BF16_SUBLANES
WIDE_WINDOW = BLOCK + BF16_SUBLANES


def _window(cpos_ref, b, e, c, w, cpad):
    base = (b * N_EXPERTS + e) * LANES + c
    a0 = jnp.minimum((cpos_ref[base] // BF16_SUBLANES) * BF16_SUBLANES, cpad - w)
    return pl.multiple_of(a0, BF16_SUBLANES), cpos_ref[base + 1] - a0 <= w


def _all_fit(cpos_ref, b, experts, c, cpad):
    fits = None
    for e in experts:
        _, ok = _window(cpos_ref, b, e, c, NARROW_WINDOW, cpad)
        fits = ok if fits is None else fits & ok
    return fits


EXPERT_GROUP = 8


def _gather_kernel(cpos_ref, hn_ref, posm_ref, x_ref, *, nsub, cpad):
    b = pl.program_id(0)
    g = pl.program_id(1)
    co = pl.program_id(2)
    experts = [g * EXPERT_GROUP + e for e in range(EXPERT_GROUP)]

    @pl.when(co == 0)
    def _():
        x_ref[...] = jnp.zeros_like(x_ref)

    def sub_chunk(i, carry):
        c = co * nsub + i
        off = pl.multiple_of(i * BLOCK, BLOCK)

        def move(w):
            rows = lax.broadcasted_iota(I32, (w, BLOCK), 0)
            starts = []
            onehots = []
            for e in range(EXPERT_GROUP):
                a0, _ = _window(cpos_ref, b, experts[e], c, w, cpad)
                starts.append(a0)
                hit = rows + a0 == posm_ref[0, 0, e:e + 1, pl.ds(off, BLOCK)]
                onehots.append(hit.astype(F32).astype(BF16))
            res = jnp.dot(jnp.concatenate(onehots, axis=0), hn_ref[0, pl.ds(off, BLOCK), :],
                          preferred_element_type=F32).astype(BF16)
            for e in range(EXPERT_GROUP):
                x_ref[0, e, pl.ds(starts[e], w), :] += res[e * w:(e + 1) * w]

        narrow = _all_fit(cpos_ref, b, experts, c, cpad)
        pl.when(narrow)(functools.partial(move, NARROW_WINDOW))
        pl.when(jnp.logical_not(narrow))(functools.partial(move, WIDE_WINDOW))
        return carry

    lax.fori_loop(0, nsub, sub_chunk, 0)


def _gather(cpos_flat, hn, posm, cpad):
    B, Lp, D = hn.shape
    E = N_EXPERTS
    ng = E // EXPERT_GROUP
    tm = _row_tile(Lp)
    posm4 = posm.reshape(B, ng, EXPERT_GROUP, Lp)
    grid_spec = pltpu.PrefetchScalarGridSpec(
        num_scalar_prefetch=1,
        grid=(B, ng, Lp // tm),
        in_specs=[pl.BlockSpec((1, tm, D), lambda b, g, c, cp: (b, c, 0)),
                  pl.BlockSpec((1, 1, EXPERT_GROUP, tm), lambda b, g, c, cp: (b, g, 0, c))],
        out_specs=pl.BlockSpec((1, EXPERT_GROUP, cpad, D), lambda b, g, c, cp: (b, g, 0, 0)),
    )
    return pl.pallas_call(
        functools.partial(_gather_kernel, nsub=tm // BLOCK, cpad=cpad),
        grid_spec=grid_spec,
        out_shape=jax.ShapeDtypeStruct((B, E, cpad, D), BF16),
        compiler_params=_params(("arbitrary", "arbitrary", "arbitrary"), 48 << 20),
        name="gather",
    )(cpos_flat, hn, posm4)


FFN_TILE = 256
FFN_BATCH = 2


def _ffn_kernel(x_ref, wg_ref, wu_ref, wd_ref, y_ref, hm_ref, wd_bf_ref, *, tf):
    f = pl.program_id(2)
    bh, _, cpad, _ = x_ref.shape
    x = x_ref[...].reshape(bh * cpad, D_MODEL)
    a = jnp.dot(x, wg_ref[0, 0].astype(BF16), preferred_element_type=F32)
    u = jnp.dot(x, wu_ref[0, 0].astype(BF16), preferred_element_type=F32)
    off = pl.multiple_of(f * tf, tf)
    hm_ref[:, pl.ds(off, tf)] = (a * jax.nn.sigmoid(a) * u).astype(BF16)
    wd_bf_ref[pl.ds(off, tf), :] = wd_ref[0, 0].astype(BF16)

    @pl.when(f == pl.num_programs(2) - 1)
    def _():
        for i in range(bh):
            y_ref[i, 0] = jnp.dot(hm_ref[i * cpad:(i + 1) * cpad, :], wd_bf_ref[...],
                                  preferred_element_type=F32).astype(BF16)


def _ffn(xe, w_gate, w_up, w_down, layer):
    B, E, cpad, D = xe.shape
    F = w_gate.shape[-1]
    tf = FFN_TILE if F % FFN_TILE == 0 else LANES
    bh = FFN_BATCH if B % FFN_BATCH == 0 else 1
    xspec = pl.BlockSpec((bh, 1, cpad, D), lambda e, b, f: (b, e, 0, 0))
    return pl.pallas_call(
        functools.partial(_ffn_kernel, tf=tf),
        grid=(E, B // bh, F // tf),
        in_specs=[xspec,
                  pl.BlockSpec((1, 1, D, tf), lambda e, b, f: (layer, e, 0, f)),
                  pl.BlockSpec((1, 1, D, tf), lambda e, b, f: (layer, e, 0, f)),
                  pl.BlockSpec((1, 1, tf, D), lambda e, b, f: (layer, e, f, 0))],
        out_specs=xspec,
        out_shape=jax.ShapeDtypeStruct(xe.shape, BF16),
        scratch_shapes=[pltpu.VMEM((bh * cpad, F), BF16), pltpu.VMEM((F, D), BF16)],
        compiler_params=_params(("parallel", "parallel", "arbitrary"), 56 << 20),
        name="ffn",
    )(xe, w_gate, w_up, w_down)


def _combine_kernel(cpos_ref, h_ref, posm_ref, gate_ref, ye_ref, o_ref, *, nsub, cpad):
    b = pl.program_id(0)
    co = pl.program_id(1)
    experts = list(range(N_EXPERTS))

    def sub_chunk(i, carry):
        c = co * nsub + i
        off = pl.multiple_of(i * BLOCK, BLOCK)

        def move(w):
            rows = lax.broadcasted_iota(I32, (w, BLOCK), 0)
            weights = []
            windows = []
            for e in experts:
                a0, _ = _window(cpos_ref, b, e, c, w, cpad)
                hit = rows + a0 == posm_ref[0, e:e + 1, pl.ds(off, BLOCK)]
                weights.append(jnp.where(hit, gate_ref[0, e:e + 1, pl.ds(off, BLOCK)], 0.0).astype(BF16))
                windows.append(ye_ref[0, e, pl.ds(a0, w), :])
            moe = lax.dot_general(jnp.concatenate(weights, axis=0), jnp.concatenate(windows, axis=0),
                                  (((0,), (0,)), ((), ())), preferred_element_type=F32)
            o_ref[0, pl.ds(off, BLOCK), :] = h_ref[0, pl.ds(off, BLOCK), :] + moe

        narrow = _all_fit(cpos_ref, b, experts, c, cpad)
        pl.when(narrow)(functools.partial(move, NARROW_WINDOW))
        pl.when(jnp.logical_not(narrow))(functools.partial(move, WIDE_WINDOW))
        return carry

    lax.fori_loop(0, nsub, sub_chunk, 0)


def _combine(cpos_flat, h1, posm, gate, ye, cpad):
    B, Lp, D = h1.shape
    E = N_EXPERTS
    tm = _row_tile(Lp)
    grid_spec = pltpu.PrefetchScalarGridSpec(
        num_scalar_prefetch=1,
        grid=(B, Lp // tm),
        in_specs=[pl.BlockSpec((1, tm, D), lambda b, c, cp: (b, c, 0)),
                  pl.BlockSpec((1, E, tm), lambda b, c, cp: (b, 0, c)),
                  pl.BlockSpec((1, E, tm), lambda b, c, cp: (b, 0, c)),
                  pl.BlockSpec((1, E, cpad, D), lambda b, c, cp: (b, 0, 0, 0),
                               pipeline_mode=pl.Buffered(1))],
        out_specs=pl.BlockSpec((1, tm, D), lambda b, c, cp: (b, c, 0)),
    )
    return pl.pallas_call(
        functools.partial(_combine_kernel, nsub=tm // BLOCK, cpad=cpad),
        grid_spec=grid_spec,
        out_shape=jax.ShapeDtypeStruct((B, Lp, D), F32),
        compiler_params=_params(("arbitrary", "arbitrary"), VMEM_LIMIT_CAP),
        name="combine",
    )(cpos_flat, h1, posm, gate, ye)


def _t5_bucket(rel):
    half = N_BUCKETS // 2
    max_exact = half // 2
    n = jnp.abs(rel)
    large = max_exact + (jnp.log(jnp.maximum(n, max_exact).astype(F32) / max_exact)
                         / math.log(MAX_DISTANCE / max_exact) * (half - max_exact)).astype(I32)
    large = jnp.minimum(large, half - 1)
    return jnp.where(rel > 0, half, 0) + jnp.where(n < max_exact, n, large)


def _block_diag_mean(n, group):
    i = jnp.arange(n)
    return jnp.where((i[:, None] // group) == (i[None, :] // group), 1.0 / group, 0.0).astype(BF16)


def kernel(x, meta_tokens, rel_bias, norm1_g, w_in, q_norm_g, k_norm_g, attn_sink, ret_decay,
           w_out, norm2_g, w_router, w_gate, w_up, w_down):
    B, seq, D = x.shape
    depth = w_in.shape[0]
    L = seq + N_META
    Lp = L + PAD_FRONT
    assert D == D_MODEL and Lp % BLOCK == 0
    cap = CAPACITY_FACTOR * L // N_EXPERTS
    cpad = -(-max(cap, WIDE_WINDOW) // BF16_SUBLANES) * BF16_SUBLANES

    hp = jnp.concatenate([jnp.zeros((B, PAD_FRONT, D), x.dtype),
                          jnp.broadcast_to(meta_tokens.astype(x.dtype)[None], (B, N_META, D)), x], axis=1)

    pos = jnp.arange(Lp, dtype=F32)
    inv = ROPE_BASE ** (-jnp.arange(0, HEAD_DIM_R, 2, dtype=F32) / HEAD_DIM_R)
    ang = pos[:, None] * inv[None]
    cc = jnp.concatenate([jnp.cos(ang), jnp.cos(ang)], axis=-1)
    ss = jnp.concatenate([-jnp.sin(ang), jnp.sin(ang)], axis=-1)
    rel = (jnp.arange(3 * BLOCK)[None, :] - BLOCK) - jnp.arange(BLOCK)[:, None]
    bucket_onehot = (_t5_bucket(rel)[None] == jnp.arange(N_BUCKETS)[:, None, None]).astype(F32)
    bias = jnp.einsum('kqs,kh->hqs', bucket_onehot, rel_bias.astype(F32), precision=lax.Precision.HIGHEST)
    bias = jnp.where((jnp.abs(rel) <= BLOCK)[None], bias, NEG)
    bias = bias.reshape(N_KV_A, 2, 2, BLOCK, 3 * BLOCK).transpose(0, 1, 3, 2, 4).reshape(N_KV_A, 2 * BLOCK, 6 * BLOCK)
    bdq = _block_diag_mean(WA_Q, HEAD_DIM_A)
    bdk = _block_diag_mean(WA_KV, HEAD_DIM_A)

    for l in range(depth):
        qg = (jnp.tile(q_norm_g[l].astype(F32), N_HEADS_A) * HEAD_DIM_A ** -0.5)[None]
        kg = jnp.tile(k_norm_g[l].astype(F32), N_KV_A)[None]
        lg2 = -jnp.exp(ret_decay[l].astype(F32))
        q, k, v, rq, rk, rv, gf, gb = _in_proj(hp, norm1_g[l][None].astype(F32), w_in[l].astype(BF16),
                                               cc, ss, qg, kg, bdq, bdk)
        sf, sb = _ret_states(lg2, rk, rv)
        mixed = _mixer(lg2, attn_sink[l].astype(F32), q, k, v, rq, rk, rv, gf, gb, sf, sb, bias)
        wr = jnp.pad(w_router[l].astype(F32), ((0, 0), (0, LANES - N_EXPERTS)))
        wr_hi = wr.astype(BF16)
        wr_lo = (wr - wr_hi.astype(F32)).astype(BF16)
        wr_split = jnp.concatenate([wr_hi, wr_lo], axis=1)
        h1, hn, aff = _out_proj(mixed, hp, w_out[l].astype(BF16), norm2_g[l][None].astype(F32), wr_split)
        posm, cpos = _route(aff, cap)
        cpos_flat = cpos.reshape(-1)
        xe = _gather(cpos_flat, hn, posm, cpad)
        ye = _ffn(xe, w_gate, w_up, w_down, l)
        hp = _combine(cpos_flat, h1, posm, aff, ye, cpad)
    return hp[:, PAD_FRONT + N_META:]
```

```python
import functools
import math

import jax
import jax.numpy as jnp
from jax import lax
from jax.experimental import pallas as pl
from jax.experimental.pallas import tpu as pltpu

D_MODEL = 1024
N_META = 16
BLOCK = 128
PAD_FRONT = BLOCK - N_META
EPS = 1e-6
N_HEADS_A = 8
N_KV_A = 2
GROUP_A = N_HEADS_A // N_KV_A
HEAD_DIM_A = 64
N_HEADS_R = 4
HEAD_DIM_R = 128
ROPE_BASE = 10000.0
N_BUCKETS = 32
MAX_DISTANCE = 128
N_EXPERTS = 16
CAPACITY_FACTOR = 2
WA_Q = N_HEADS_A * HEAD_DIM_A
WA_KV = N_KV_A * HEAD_DIM_A
WR = N_HEADS_R * HEAD_DIM_R
D_IN = WA_Q + 2 * WA_KV + 5 * WR
NEG = -1e30
LOG2E = math.log2(math.e)

LANES = 128
BF16_SUBLANES = 16
VMEM_LIMIT_CAP = 60000 * 1024

F32 = jnp.float32
BF16 = jnp.bfloat16
I32 = jnp.int32


def _params(semantics, vmem_bytes, **extra):
    return pltpu.CompilerParams(dimension_semantics=semantics,
                                vmem_limit_bytes=min(int(vmem_bytes), VMEM_LIMIT_CAP), **extra)


def _row_tile(lp):
    best = BLOCK
    for t in range(BLOCK, 1024 + 1, BLOCK):
        if lp % t == 0:
            best = t
    return best


def _split_dot(x, w_ref):
    hi = x.astype(BF16)
    lo = (x - hi.astype(F32)).astype(BF16)
    w = w_ref[...]
    return (jnp.dot(hi, w, preferred_element_type=F32) + jnp.dot(lo, w, preferred_element_type=F32))


def _in_proj_kernel(h_ref, g1_ref, w_ref, cc_ref, ss_ref, qg_ref, kg_ref, bdq_ref, bdk_ref,
                    q_ref, k_ref, v_ref, rq_ref, rk_ref, rv_ref, gf_ref, gb_ref, *, tm):
    j = pl.program_id(1)
    x = h_ref[0]
    ms = jnp.mean(x * x, axis=-1, keepdims=True)
    xn = x * lax.rsqrt(ms + EPS) * g1_ref[...]
    row = j * tm + lax.broadcasted_iota(I32, (tm, 1), 0)
    xn = jnp.where(row >= PAD_FRONT, xn, 0.0).astype(BF16)

    def proj(lo, n):
        return jnp.dot(xn, w_ref[:, lo:lo + n], preferred_element_type=F32)

    q = proj(0, WA_Q)
    q_ms = jnp.dot((q * q).astype(BF16), bdq_ref[...], preferred_element_type=F32)
    q_ref[0] = (q * lax.rsqrt(q_ms + EPS) * qg_ref[...]).astype(BF16)
    k = proj(WA_Q, WA_KV)
    k = k * lax.rsqrt(_split_dot(k * k, bdk_ref) + EPS) * kg_ref[...]
    v = proj(WA_Q + WA_KV, WA_KV)
    lower = lax.broadcasted_iota(I32, (1, WA_KV), 1) < HEAD_DIM_A
    for t, ref in ((k, k_ref), (v, v_ref)):
        head0_lo = jnp.where(lower, t, 0.0)
        head1_hi = jnp.where(lower, 0.0, t)
        variants = (head0_lo, pltpu.roll(head0_lo, HEAD_DIM_A, 1), pltpu.roll(head1_hi, HEAD_DIM_A, 1), head1_hi)
        for i, var in enumerate(variants):
            ref[0, :, i * WA_KV:(i + 1) * WA_KV] = var.astype(BF16)

    cc = cc_ref[...]
    ss = ss_ref[...]
    base = WA_Q + 2 * WA_KV
    for off, ref, scale in ((base, rq_ref, 1.0), (base + WR, rk_ref, HEAD_DIM_R ** -0.5)):
        r = proj(off, WR)
        for hh in range(N_HEADS_R):
            seg = r[:, hh * HEAD_DIM_R:(hh + 1) * HEAD_DIM_R]
            rot = seg * cc + pltpu.roll(seg, HEAD_DIM_R // 2, 1) * ss
            ref[0, :, hh * HEAD_DIM_R:(hh + 1) * HEAD_DIM_R] = (rot * scale).astype(BF16)
    rv_ref[0] = proj(base + 2 * WR, WR).astype(BF16)
    gf_ref[0] = proj(base + 3 * WR, WR).astype(BF16)
    gb_ref[0] = proj(base + 4 * WR, WR).astype(BF16)


def _in_proj(hp, g1, w_in_bf, cc, ss, qg, kg, bdq, bdk):
    B, Lp, D = hp.shape
    tm = _row_tile(Lp)
    const = lambda b, j: (0, 0)
    row = lambda n: pl.BlockSpec((1, tm, n), lambda b, j: (b, j, 0))
    widths = (WA_Q, 2 * N_KV_A * WA_KV, 2 * N_KV_A * WA_KV, WR, WR, WR, WR, WR)
    return pl.pallas_call(
        functools.partial(_in_proj_kernel, tm=tm),
        grid=(B, Lp // tm),
        in_specs=[row(D),
                  pl.BlockSpec((1, D), const),
                  pl.BlockSpec((D, D_IN), const),
                  pl.BlockSpec((tm, HEAD_DIM_R), lambda b, j: (j, 0)),
                  pl.BlockSpec((tm, HEAD_DIM_R), lambda b, j: (j, 0)),
                  pl.BlockSpec((1, WA_Q), const),
                  pl.BlockSpec((1, WA_KV), const),
                  pl.BlockSpec((WA_Q, WA_Q), const),
                  pl.BlockSpec((WA_KV, WA_KV), const)],
        out_specs=[row(n) for n in widths],
        out_shape=[jax.ShapeDtypeStruct((B, Lp, n), BF16) for n in widths],
        compiler_params=_params(("parallel", "parallel"), 48 << 20),
        name="in_proj",
    )(hp, g1, w_in_bf, cc, ss, qg, kg, bdq, bdk)


def _ret_state_kernel(lg_ref, kf_ref, vf_ref, kb_ref, vb_ref, sf_ref, sb_ref, st_ref, *, nsub):
    t = pl.program_id(0)

    @pl.when(t == 0)
    def _():
        st_ref[...] = jnp.zeros_like(st_ref)

    idx = lax.broadcasted_iota(I32, (BLOCK, 1), 0).astype(F32)
    ones_row = jnp.ones((1, HEAD_DIM_R), F32)
    for d, (k_ref, v_ref, s_out) in enumerate(((kf_ref, vf_ref, sf_ref), (kb_ref, vb_ref, sb_ref))):
        for h in range(N_HEADS_R):
            lg = lg_ref[d, h]
            sl = slice(h * HEAD_DIM_R, (h + 1) * HEAD_DIM_R)
            zeta = jnp.exp(lg * ((BLOCK - 1.0 - idx) if d == 0 else idx))
            chunk_decay = jnp.exp(lg * float(BLOCK) * ones_row)
            for b in range(st_ref.shape[0]):
                st = st_ref[b, d, h]
                for step in range(nsub):
                    j = step if d == 0 else nsub - 1 - step
                    rows = slice(j * BLOCK, (j + 1) * BLOCK)
                    s_out[b, j, h] = st.astype(BF16)
                    kz = (k_ref[b, rows, sl].astype(F32) * zeta).astype(BF16)
                    kv = lax.dot_general(kz, v_ref[b, rows, sl], (((0,), (0,)), ((), ())),
                                         preferred_element_type=F32)
                    st = st * chunk_decay + kv
                st_ref[b, d, h] = st


def _ret_states(lg2, rk, rv):
    B, Lp, _ = rk.shape
    tm = _row_tile(Lp)
    nsub = tm // BLOCK
    nt = Lp // tm
    fwd = pl.BlockSpec((B, tm, WR), lambda t: (0, t, 0))
    bwd = pl.BlockSpec((B, tm, WR), lambda t: (0, nt - 1 - t, 0))
    st_shape = (B, nsub, N_HEADS_R, HEAD_DIM_R, HEAD_DIM_R)
    return pl.pallas_call(
        functools.partial(_ret_state_kernel, nsub=nsub),
        grid=(nt,),
        in_specs=[pl.BlockSpec(memory_space=pltpu.SMEM), fwd, fwd, bwd, bwd],
        out_specs=[pl.BlockSpec(st_shape, lambda t: (0, t, 0, 0, 0)),
                   pl.BlockSpec(st_shape, lambda t: (0, nt - 1 - t, 0, 0, 0))],
        out_shape=[jax.ShapeDtypeStruct((B, Lp // BLOCK) + st_shape[2:], BF16)] * 2,
        scratch_shapes=[pltpu.VMEM((B, 2, N_HEADS_R, HEAD_DIM_R, HEAD_DIM_R), F32)],
        compiler_params=_params(("arbitrary",), 48 << 20),
        name="ret_state",
    )(lg2, rk, rv, rk, rv)


def _mixer_kernel(lg_ref, sink_ref, q_ref, kp_ref, kc_ref, kn_ref, vp_ref, vc_ref, vn_ref,
                  rq_ref, rk_ref, rv_ref, gf_ref, gb_ref, sf_ref, sb_ref, bias_ref,
                  o_ref, dec_ref, xi_ref, kbuf_ref, vbuf_ref, *, nblocks, tm):
    b = pl.program_id(0)
    c = pl.program_id(1)
    nsub = tm // BLOCK
    kv_w = 3 * BLOCK

    @pl.when((b == 0) & (c == 0))
    def _():
        ii = lax.broadcasted_iota(I32, (BLOCK, BLOCK), 0).astype(F32)
        jj = lax.broadcasted_iota(I32, (BLOCK, BLOCK), 1).astype(F32)
        for d in range(2):
            diff = (ii - jj) if d == 0 else (jj - ii)
            reach = (ii + 1.0) if d == 0 else (float(BLOCK) - ii)
            for h in range(N_HEADS_R):
                lg = lg_ref[d, h]
                dec_ref[d, h] = jnp.where(diff >= 0, jnp.exp(lg * jnp.maximum(diff, 0.0)), 0.0)
                xi_ref[d, h] = jnp.exp(lg * reach)

    for buf, (p_ref, c_ref, n_ref) in ((kbuf_ref, (kp_ref, kc_ref, kn_ref)), (vbuf_ref, (vp_ref, vc_ref, vn_ref))):
        buf[0:BLOCK] = p_ref[0]
        buf[BLOCK:BLOCK + tm] = c_ref[0]
        buf[BLOCK + tm:] = n_ref[0]

    lower_lanes = lax.broadcasted_iota(I32, (1, 2 * HEAD_DIM_A), 1) < HEAD_DIM_A
    upper_rows = lax.broadcasted_iota(I32, (2 * BLOCK, 1), 0) >= BLOCK

    def block(i, carry):
        r0 = pl.multiple_of(i * BLOCK, BLOCK)
        rows = pl.ds(r0, BLOCK)
        win = pl.ds(r0, kv_w)

        n = c * nsub + i
        variant = jnp.where(n < 2, n, jnp.where(n == nblocks - 1, 3, 2))
        for kh in range(N_KV_A):
            g0 = slice(2 * kh * WA_KV, (2 * kh + 1) * WA_KV)
            g1 = slice((2 * kh + 1) * WA_KV, (2 * kh + 2) * WA_KV)
            q2 = jnp.concatenate([q_ref[0, rows, g0], q_ref[0, rows, g1]], axis=0)
            kcat = jnp.concatenate([kbuf_ref[win, g0], kbuf_ref[win, g1]], axis=0)
            vcat = jnp.concatenate([vbuf_ref[win, g0], vbuf_ref[win, g1]], axis=0)
            s = lax.dot_general(q2, kcat, (((1,), (1,)), ((), ())), preferred_element_type=F32)
            s = s + bias_ref[variant, kh]
            probs = []
            denoms = []
            for half in range(2):
                sh = s[:, half * kv_w:(half + 1) * kv_w]
                sk = jnp.where(upper_rows, sink_ref[4 * kh + 2 + half], sink_ref[4 * kh + half])
                m = jnp.maximum(jnp.max(sh, axis=-1, keepdims=True), sk)
                p = jnp.exp2(sh - m)
                denoms.append(jnp.sum(p, axis=-1, keepdims=True) + jnp.exp2(sk - m))
                probs.append(p.astype(BF16))
            o = jnp.dot(jnp.concatenate(probs, axis=1), vcat, preferred_element_type=F32)
            o = o / jnp.where(lower_lanes, denoms[0], denoms[1])
            o_ref[0, rows, g0] = o[:BLOCK].astype(BF16)
            o_ref[0, rows, g1] = o[BLOCK:].astype(BF16)

        return carry

    def ret_block(i, carry):
        r0 = pl.multiple_of(i * BLOCK, BLOCK)
        rows = pl.ds(r0, BLOCK)
        for h in range(N_HEADS_R):
            sl = slice(h * HEAD_DIM_R, (h + 1) * HEAD_DIM_R)
            qh = rq_ref[0, rows, sl]
            vh = rv_ref[0, rows, sl]
            qk = lax.dot_general(qh, rk_ref[0, rows, sl], (((1,), (1,)), ((), ())),
                                 preferred_element_type=F32)
            qf = qh.astype(F32)
            acc = None
            for d, (st_ref, g_ref) in enumerate(((sf_ref, gf_ref), (sb_ref, gb_ref))):
                lhs = jnp.concatenate([(qk * dec_ref[d, h]).astype(BF16), (qf * xi_ref[d, h]).astype(BF16)], axis=1)
                rhs = jnp.concatenate([vh, st_ref[0, i, h]], axis=0)
                y = jnp.dot(lhs, rhs, preferred_element_type=F32)
                mu = jnp.mean(y, axis=-1, keepdims=True)
                yc = y - mu
                var = jnp.mean(yc * yc, axis=-1, keepdims=True)
                gate = g_ref[0, rows, sl].astype(F32)
                term = gate * jax.nn.sigmoid(gate) * (yc * lax.rsqrt(var + EPS))
                acc = term if acc is None else acc + term
            o_ref[0, rows, WA_Q + h * HEAD_DIM_R:WA_Q + (h + 1) * HEAD_DIM_R] = acc.astype(BF16)
        return carry

    for i in range(nsub):
        block(i, 0)
    for i in range(nsub):
        ret_block(i, 0)


def _mixer(lg2, sink, q, k4, v4, rq, rk, rv, gf, gb, sf, sb, bias2):
    B, Lp, _ = q.shape
    nb = Lp // BLOCK
    assert nb >= 3
    tm = _row_tile(Lp)
    nsub = tm // BLOCK
    kvw = k4.shape[-1]
    cur = lambda n_: pl.BlockSpec((1, tm, n_), lambda b, c: (b, c, 0))
    prev = pl.BlockSpec((1, BLOCK, kvw), lambda b, c: (b, jnp.maximum(c * nsub - 1, 0), 0))
    nxt = pl.BlockSpec((1, BLOCK, kvw), lambda b, c: (b, jnp.minimum((c + 1) * nsub, nb - 1), 0))
    st = pl.BlockSpec((1, nsub, N_HEADS_R, HEAD_DIM_R, HEAD_DIM_R), lambda b, c: (b, c, 0, 0, 0))
    smem = pl.BlockSpec(memory_space=pltpu.SMEM)
    tab = pltpu.VMEM((2, N_HEADS_R, BLOCK, HEAD_DIM_R), F32)
    kvbuf = pltpu.VMEM((tm + 2 * BLOCK, kvw), BF16)
    return pl.pallas_call(
        functools.partial(_mixer_kernel, nblocks=nb, tm=tm),
        grid=(B, Lp // tm),
        in_specs=[smem, smem, cur(WA_Q), prev, cur(kvw), nxt, prev, cur(kvw), nxt,
                  cur(WR), cur(WR), cur(WR), cur(WR), cur(WR), st, st,
                  pl.BlockSpec(bias2.shape, lambda b, c: (0, 0, 0, 0))],
        out_specs=pl.BlockSpec((1, tm, D_MODEL), lambda b, c: (b, c, 0)),
        out_shape=jax.ShapeDtypeStruct((B, Lp, D_MODEL), BF16),
        scratch_shapes=[tab, tab, kvbuf, kvbuf],
        compiler_params=_params(("arbitrary", "arbitrary"), 40 << 20),
        name="mixer",
    )(lg2, sink, q, k4, k4, k4, v4, v4, v4, rq, rk, rv, gf, gb, sf, sb, bias2)


def _out_proj_kernel(mix_ref, h_ref, w_ref, g2_ref, wr_ref, h1_ref, hn_ref, aff_ref, *, tm):
    j = pl.program_id(1)
    h1 = h_ref[0] + jnp.dot(mix_ref[0], w_ref[...], preferred_element_type=F32)
    h1_ref[0] = h1
    ms = jnp.mean(h1 * h1, axis=-1, keepdims=True)
    xn = h1 * lax.rsqrt(ms + EPS) * g2_ref[...]
    hn_ref[0] = xn.astype(BF16)
    hi = xn.astype(BF16)
    lo = (xn - hi.astype(F32)).astype(BF16)
    both = jnp.dot(hi, wr_ref[...], preferred_element_type=F32)
    logits = (both[:, :LANES] + both[:, LANES:]
              + jnp.dot(lo, wr_ref[:, :LANES], preferred_element_type=F32))
    lt = logits.T[:N_EXPERTS]
    mx = jnp.max(lt, axis=0, keepdims=True)
    ex = jnp.exp(lt - mx)
    aff = ex / jnp.sum(ex, axis=0, keepdims=True)
    col = j * tm + lax.broadcasted_iota(I32, (1, tm), 1)
    aff_ref[0] = jnp.where(col >= PAD_FRONT, aff, -1.0)


def _out_proj(mixed, hp, w_out_bf, g2, wr_split):
    B, Lp, D = hp.shape
    tm = _row_tile(Lp)
    const = lambda b, j: (0, 0)
    row = lambda: pl.BlockSpec((1, tm, D), lambda b, j: (b, j, 0))
    return pl.pallas_call(
        functools.partial(_out_proj_kernel, tm=tm),
        grid=(B, Lp // tm),
        in_specs=[row(), row(), pl.BlockSpec((D, D), const), pl.BlockSpec((1, D), const),
                  pl.BlockSpec((D, 2 * LANES), const)],
        out_specs=[row(), row(), pl.BlockSpec((1, N_EXPERTS, tm), lambda b, j: (b, 0, j))],
        out_shape=[jax.ShapeDtypeStruct((B, Lp, D), F32),
                   jax.ShapeDtypeStruct((B, Lp, D), BF16),
                   jax.ShapeDtypeStruct((B, N_EXPERTS, Lp), F32)],
        compiler_params=_params(("parallel", "parallel"), 32 << 20),
        name="out_proj",
    )(mixed, hp, w_out_bf, g2, wr_split)


def _route_kernel(aff_ref, posm_ref, cpos_ref, *, cap, nb):
    E = aff_ref.shape[0]

    def keys(sl=slice(None)):
        return pltpu.bitcast(aff_ref[:, sl], I32)

    def count_ge(t):
        return jnp.sum((keys() >= t).astype(I32), axis=-1, keepdims=True)

    def search(_, carry):
        lo, hi = carry
        mid = lo + (hi - lo) // 2
        ok = count_ge(mid) >= cap
        return jnp.where(ok, mid, lo), jnp.where(ok, hi, mid)

    lo0 = jnp.zeros((E, 1), I32)
    hi0 = jnp.full((E, 1), 0x3F800001, I32)
    thr, _ = lax.fori_loop(0, 31, search, (lo0, hi0))
    n_gt = jnp.sum((keys() > thr).astype(F32), axis=-1, keepdims=True)
    need_eq = float(cap) - n_gt

    jr = lax.broadcasted_iota(I32, (LANES, LANES), 0)
    jc = lax.broadcasted_iota(I32, (LANES, LANES), 1)
    upper = (jr < jc).astype(F32).astype(BF16)
    lane = lax.broadcasted_iota(I32, (E, LANES), 1)

    def chunk(c, carry):
        cg, ce, cp = carry
        off = pl.multiple_of(c * LANES, LANES)
        kk = pltpu.bitcast(aff_ref[:, pl.ds(off, LANES)], I32)
        gt = kk > thr
        eq = kk == thr
        both = jnp.concatenate([gt, eq], axis=0).astype(F32)
        ex = jnp.dot(both.astype(BF16), upper, preferred_element_type=F32)
        exg = ex[:E] + cg
        exe = ex[E:] + ce
        sel = gt | (eq & (exe < need_eq))
        pos = exg + jnp.minimum(exe, need_eq)
        posm_ref[:, pl.ds(off, LANES)] = jnp.where(sel, pos, -1.0).astype(I32)
        cp = jnp.where(lane == c, (cg + jnp.minimum(ce, need_eq)).astype(I32), cp)
        tot = jnp.sum(both, axis=-1, keepdims=True)
        return cg + tot[:E], ce + tot[E:], cp

    zero = jnp.zeros((E, 1), F32)
    cg, ce, cp = lax.fori_loop(0, nb, chunk, (zero, zero, jnp.zeros((E, LANES), I32)))
    cpos_ref[...] = jnp.where(lane == nb, (cg + jnp.minimum(ce, need_eq)).astype(I32), cp)


def _route(aff_t, cap):
    B, E, Lp = aff_t.shape
    nb = Lp // LANES
    assert nb + 1 <= LANES
    posm, cpos = pl.pallas_call(
        functools.partial(_route_kernel, cap=cap, nb=nb),
        grid=(1,),
        in_specs=[pl.BlockSpec((B * E, Lp), lambda i: (0, 0))],
        out_specs=[pl.BlockSpec((B * E, Lp), lambda i: (0, 0)),
                   pl.BlockSpec((B * E, LANES), lambda i: (0, 0))],
        out_shape=[jax.ShapeDtypeStruct((B * E, Lp), I32),
                   jax.ShapeDtypeStruct((B * E, LANES), I32)],
        compiler_params=_params(("arbitrary",), 32 << 20),
        name="route",
    )(aff_t.reshape(B * E, Lp))
    return posm.reshape(B, E, Lp), cpos


NARROW_WINDOW = 3 * BF16_SUBLANES
WIDE_WINDOW = BLOCK + BF16_SUBLANES


def _window(cpos_ref, b, e, c, w, cpad):
    base = (b * N_EXPERTS + e) * LANES + c
    a0 = jnp.minimum((cpos_ref[base] // BF16_SUBLANES) * BF16_SUBLANES, cpad - w)
    return pl.multiple_of(a0, BF16_SUBLANES), cpos_ref[base + 1] - a0 <= w


def _all_fit(cpos_ref, b, experts, c, cpad):
    fits = None
    for e in experts:
        _, ok = _window(cpos_ref, b, e, c, NARROW_WINDOW, cpad)
        fits = ok if fits is None else fits & ok
    return fits


EXPERT_GROUP = 8


def _gather_kernel(cpos_ref, hn_ref, posm_ref, x_ref, *, nsub, cpad):
    b = pl.program_id(0)
    g = pl.program_id(1)
    co = pl.program_id(2)
    experts = [g * EXPERT_GROUP + e for e in range(EXPERT_GROUP)]

    @pl.when(co == 0)
    def _():
        x_ref[...] = jnp.zeros_like(x_ref)

    def sub_chunk(i, carry):
        c = co * nsub + i
        off = pl.multiple_of(i * BLOCK, BLOCK)

        def move(w):
            rows = lax.broadcasted_iota(I32, (w, BLOCK), 0)
            starts = []
            onehots = []
            for e in range(EXPERT_GROUP):
                a0, _ = _window(cpos_ref, b, experts[e], c, w, cpad)
                starts.append(a0)
                hit = rows + a0 == posm_ref[0, 0, e:e + 1, pl.ds(off, BLOCK)]
                onehots.append(hit.astype(F32).astype(BF16))
            res = jnp.dot(jnp.concatenate(onehots, axis=0), hn_ref[0, pl.ds(off, BLOCK), :],
                          preferred_element_type=F32).astype(BF16)
            for e in range(EXPERT_GROUP):
                x_ref[0, e, pl.ds(starts[e], w), :] += res[e * w:(e + 1) * w]

        narrow = _all_fit(cpos_ref, b, experts, c, cpad)
        pl.when(narrow)(functools.partial(move, NARROW_WINDOW))
        pl.when(jnp.logical_not(narrow))(functools.partial(move, WIDE_WINDOW))
        return carry

    lax.fori_loop(0, nsub, sub_chunk, 0)


def _gather(cpos_flat, hn, posm, cpad):
    B, Lp, D = hn.shape
    E = N_EXPERTS
    ng = E // EXPERT_GROUP
    tm = _row_tile(Lp)
    posm4 = posm.reshape(B, ng, EXPERT_GROUP, Lp)
    grid_spec = pltpu.PrefetchScalarGridSpec(
        num_scalar_prefetch=1,
        grid=(B, ng, Lp // tm),
        in_specs=[pl.BlockSpec((1, tm, D), lambda b, g, c, cp: (b, c, 0)),
                  pl.BlockSpec((1, 1, EXPERT_GROUP, tm), lambda b, g, c, cp: (b, g, 0, c))],
        out_specs=pl.BlockSpec((1, EXPERT_GROUP, cpad, D), lambda b, g, c, cp: (b, g, 0, 0)),
    )
    return pl.pallas_call(
        functools.partial(_gather_kernel, nsub=tm // BLOCK, cpad=cpad),
        grid_spec=grid_spec,
        out_shape=jax.ShapeDtypeStruct((B, E, cpad, D), BF16),
        compiler_params=_params(("arbitrary", "arbitrary", "arbitrary"), 48 << 20),
        name="gather",
    )(cpos_flat, hn, posm4)


FFN_TILE = 256
FFN_BATCH = 2


def _ffn_kernel(x_ref, wg_ref, wu_ref, wd_ref, y_ref, hm_ref, wd_bf_ref, *, tf):
    f = pl.program_id(2)
    bh, _, cpad, _ = x_ref.shape
    x = x_ref[...].reshape(bh * cpad, D_MODEL)
    a = jnp.dot(x, wg_ref[0, 0].astype(BF16), preferred_element_type=F32)
    u = jnp.dot(x, wu_ref[0, 0].astype(BF16), preferred_element_type=F32)
    off = pl.multiple_of(f * tf, tf)
    hm_ref[:, pl.ds(off, tf)] = (a * jax.nn.sigmoid(a) * u).astype(BF16)
    wd_bf_ref[pl.ds(off, tf), :] = wd_ref[0, 0].astype(BF16)

    @pl.when(f == pl.num_programs(2) - 1)
    def _():
        for i in range(bh):
            y_ref[i, 0] = jnp.dot(hm_ref[i * cpad:(i + 1) * cpad, :], wd_bf_ref[...],
                                  preferred_element_type=F32).astype(BF16)


def _ffn(xe, w_gate, w_up, w_down, layer):
    B, E, cpad, D = xe.shape
    F = w_gate.shape[-1]
    tf = FFN_TILE if F % FFN_TILE == 0 else LANES
    bh = FFN_BATCH if B % FFN_BATCH == 0 else 1
    xspec = pl.BlockSpec((bh, 1, cpad, D), lambda e, b, f: (b, e, 0, 0))
    return pl.pallas_call(
        functools.partial(_ffn_kernel, tf=tf),
        grid=(E, B // bh, F // tf),
        in_specs=[xspec,
                  pl.BlockSpec((1, 1, D, tf), lambda e, b, f: (layer, e, 0, f)),
                  pl.BlockSpec((1, 1, D, tf), lambda e, b, f: (layer, e, 0, f)),
                  pl.BlockSpec((1, 1, tf, D), lambda e, b, f: (layer, e, f, 0))],
        out_specs=xspec,
        out_shape=jax.ShapeDtypeStruct(xe.shape, BF16),
        scratch_shapes=[pltpu.VMEM((bh * cpad, F), BF16), pltpu.VMEM((F, D), BF16)],
        compiler_params=_params(("parallel", "parallel", "arbitrary"), 56 << 20),
        name="ffn",
    )(xe, w_gate, w_up, w_down)


def _combine_kernel(cpos_ref, h_ref, posm_ref, gate_ref, ye_ref, o_ref, *, nsub, cpad):
    b = pl.program_id(0)
    co = pl.program_id(1)
    experts = list(range(N_EXPERTS))

    def sub_chunk(i, carry):
        c = co * nsub + i
        off = pl.multiple_of(i * BLOCK, BLOCK)

        def move(w):
            rows = lax.broadcasted_iota(I32, (w, BLOCK), 0)
            weights = []
            windows = []
            for e in experts:
                a0, _ = _window(cpos_ref, b, e, c, w, cpad)
                hit = rows + a0 == posm_ref[0, e:e + 1, pl.ds(off, BLOCK)]
                weights.append(jnp.where(hit, gate_ref[0, e:e + 1, pl.ds(off, BLOCK)], 0.0).astype(BF16))
                windows.append(ye_ref[0, e, pl.ds(a0, w), :])
            moe = lax.dot_general(jnp.concatenate(weights, axis=0), jnp.concatenate(windows, axis=0),
                                  (((0,), (0,)), ((), ())), preferred_element_type=F32)
            o_ref[0, pl.ds(off, BLOCK), :] = h_ref[0, pl.ds(off, BLOCK), :] + moe

        narrow = _all_fit(cpos_ref, b, experts, c, cpad)
        pl.when(narrow)(functools.partial(move, NARROW_WINDOW))
        pl.when(jnp.logical_not(narrow))(functools.partial(move, WIDE_WINDOW))
        return carry

    lax.fori_loop(0, nsub, sub_chunk, 0)


def _combine(cpos_flat, h1, posm, gate, ye, cpad):
    B, Lp, D = h1.shape
    E = N_EXPERTS
    tm = _row_tile(Lp)
    grid_spec = pltpu.PrefetchScalarGridSpec(
        num_scalar_prefetch=1,
        grid=(B, Lp // tm),
        in_specs=[pl.BlockSpec((1, tm, D), lambda b, c, cp: (b, c, 0)),
                  pl.BlockSpec((1, E, tm), lambda b, c, cp: (b, 0, c)),
                  pl.BlockSpec((1, E, tm), lambda b, c, cp: (b, 0, c)),
                  pl.BlockSpec((1, E, cpad, D), lambda b, c, cp: (b, 0, 0, 0),
                               pipeline_mode=pl.Buffered(1))],
        out_specs=pl.BlockSpec((1, tm, D), lambda b, c, cp: (b, c, 0)),
    )
    return pl.pallas_call(
        functools.partial(_combine_kernel, nsub=tm // BLOCK, cpad=cpad),
        grid_spec=grid_spec,
        out_shape=jax.ShapeDtypeStruct((B, Lp, D), F32),
        compiler_params=_params(("arbitrary", "arbitrary"), VMEM_LIMIT_CAP),
        name="combine",
    )(cpos_flat, h1, posm, gate, ye)


def _t5_bucket(rel):
    half = N_BUCKETS // 2
    max_exact = half // 2
    n = jnp.abs(rel)
    large = max_exact + (jnp.log(jnp.maximum(n, max_exact).astype(F32) / max_exact)
                         / math.log(MAX_DISTANCE / max_exact) * (half - max_exact)).astype(I32)
    large = jnp.minimum(large, half - 1)
    return jnp.where(rel > 0, half, 0) + jnp.where(n < max_exact, n, large)


def _block_diag_mean(n, group):
    i = jnp.arange(n)
    return jnp.where((i[:, None] // group) == (i[None, :] // group), 1.0 / group, 0.0).astype(BF16)


def kernel(x, meta_tokens, rel_bias, norm1_g, w_in, q_norm_g, k_norm_g, attn_sink, ret_decay,
           w_out, norm2_g, w_router, w_gate, w_up, w_down):
    B, seq, D = x.shape
    depth = w_in.shape[0]
    L = seq + N_META
    Lp = L + PAD_FRONT
    assert D == D_MODEL and Lp % BLOCK == 0
    cap = CAPACITY_FACTOR * L // N_EXPERTS
    cpad = -(-max(cap, WIDE_WINDOW) // BF16_SUBLANES) * BF16_SUBLANES

    hp = jnp.concatenate([jnp.zeros((B, PAD_FRONT, D), x.dtype),
                          jnp.broadcast_to(meta_tokens.astype(x.dtype)[None], (B, N_META, D)), x], axis=1)

    pos = jnp.arange(Lp, dtype=F32)
    inv = ROPE_BASE ** (-jnp.arange(0, HEAD_DIM_R, 2, dtype=F32) / HEAD_DIM_R)
    ang = pos[:, None] * inv[None]
    cc = jnp.concatenate([jnp.cos(ang), jnp.cos(ang)], axis=-1)
    ss = jnp.concatenate([-jnp.sin(ang), jnp.sin(ang)], axis=-1)
    rel = (jnp.arange(3 * BLOCK)[None, :] - BLOCK) - jnp.arange(BLOCK)[:, None]
    bucket_onehot = (_t5_bucket(rel)[None] == jnp.arange(N_BUCKETS)[:, None, None]).astype(F32)
    bias = jnp.einsum('kqs,kh->hqs', bucket_onehot, rel_bias.astype(F32), precision=lax.Precision.HIGHEST)
    bias = jnp.where((jnp.abs(rel) <= BLOCK)[None], bias * LOG2E, NEG)
    key = jnp.arange(3 * BLOCK)
    bias = jnp.stack([jnp.where(key >= BLOCK + PAD_FRONT, bias, NEG), jnp.where(key >= PAD_FRONT, bias, NEG),
                      bias, jnp.where(key < 2 * BLOCK, bias, NEG)])
    bias = bias.reshape(4, N_KV_A, 2, 2, BLOCK, 3 * BLOCK).transpose(0, 1, 2, 4, 3, 5)
    bias = bias.reshape(4, N_KV_A, 2 * BLOCK, 6 * BLOCK)
    bdq = _block_diag_mean(WA_Q, HEAD_DIM_A)
    bdk = _block_diag_mean(WA_KV, HEAD_DIM_A)

    for l in range(depth):
        qg = (jnp.tile(q_norm_g[l].astype(F32), N_HEADS_A) * (HEAD_DIM_A ** -0.5 * LOG2E))[None]
        kg = jnp.tile(k_norm_g[l].astype(F32), N_KV_A)[None]
        lg2 = -jnp.exp(ret_decay[l].astype(F32))
        q, k, v, rq, rk, rv, gf, gb = _in_proj(hp, norm1_g[l][None].astype(F32), w_in[l].astype(BF16),
                                               cc, ss, qg, kg, bdq, bdk)
        sf, sb = _ret_states(lg2, rk, rv)
        mixed = _mixer(lg2, attn_sink[l].astype(F32) * LOG2E, q, k, v, rq, rk, rv, gf, gb, sf, sb, bias)
        wr = jnp.pad(w_router[l].astype(F32), ((0, 0), (0, LANES - N_EXPERTS)))
        wr_hi = wr.astype(BF16)
        wr_lo = (wr - wr_hi.astype(F32)).astype(BF16)
        wr_split = jnp.concatenate([wr_hi, wr_lo], axis=1)
        h1, hn, aff = _out_proj(mixed, hp, w_out[l].astype(BF16), norm2_g[l][None].astype(F32), wr_split)
        posm, cpos = _route(aff, cap)
        cpos_flat = cpos.reshape(-1)
        xe = _gather(cpos_flat, hn, posm, cpad)
        ye = _ffn(xe, w_gate, w_up, w_down, l)
        hp = _combine(cpos_flat, h1, posm, aff, ye, cpad)
    return hp[:, PAD_FRONT + N_META:]
```

```python
import functools
import math

import jax
import jax.numpy as jnp
from jax import lax
from jax.experimental import pallas as pl
from jax.experimental.pallas import tpu as pltpu

D_MODEL = 1024
N_META = 16
BLOCK = 128
PAD_FRONT = BLOCK - N_META
EPS = 1e-6
N_HEADS_A = 8
N_KV_A = 2
GROUP_A = N_HEADS_A // N_KV_A
HEAD_DIM_A = 64
N_HEADS_R = 4
HEAD_DIM_R = 128
ROPE_BASE = 10000.0
N_BUCKETS = 32
MAX_DISTANCE = 128
N_EXPERTS = 16
CAPACITY_FACTOR = 2
WA_Q = N_HEADS_A * HEAD_DIM_A
WA_KV = N_KV_A * HEAD_DIM_A
WR = N_HEADS_R * HEAD_DIM_R
D_IN = WA_Q + 2 * WA_KV + 5 * WR
NEG = -1e30
LOG2E = math.log2(math.e)

LANES = 128
BF16_SUBLANES = 16
VMEM_LIMIT_CAP = 60000 * 1024

F32 = jnp.float32
BF16 = jnp.bfloat16
I32 = jnp.int32


def _params(semantics, vmem_bytes, **extra):
    return pltpu.CompilerParams(dimension_semantics=semantics,
                                vmem_limit_bytes=min(int(vmem_bytes), VMEM_LIMIT_CAP), **extra)


def _row_tile(lp):
    best = BLOCK
    for t in range(BLOCK, 1024 + 1, BLOCK):
        if lp % t == 0:
            best = t
    return best


def _split_dot(x, w_ref):
    hi = x.astype(BF16)
    lo = (x - hi.astype(F32)).astype(BF16)
    w = w_ref[...]
    return (jnp.dot(hi, w, preferred_element_type=F32) + jnp.dot(lo, w, preferred_element_type=F32))


def _in_proj_kernel(h_ref, g1_ref, w_ref, cc_ref, ss_ref, qg_ref, kg_ref, bdq_ref, bdk_ref,
                    q_ref, k_ref, v_ref, rq_ref, rk_ref, rv_ref, gf_ref, gb_ref, *, tm):
    j = pl.program_id(1)
    x = h_ref[0]
    ms = jnp.mean(x * x, axis=-1, keepdims=True)
    xn = x * lax.rsqrt(ms + EPS) * g1_ref[...]
    row = j * tm + lax.broadcasted_iota(I32, (tm, 1), 0)
    xn = jnp.where(row >= PAD_FRONT, xn, 0.0).astype(BF16)

    def proj(lo, n):
        return jnp.dot(xn, w_ref[:, lo:lo + n], preferred_element_type=F32)

    q = proj(0, WA_Q)
    q_ms = jnp.dot((q * q).astype(BF16), bdq_ref[...], preferred_element_type=F32)
    q_ref[0] = (q * lax.rsqrt(q_ms + EPS) * qg_ref[...]).astype(BF16)
    k = proj(WA_Q, WA_KV)
    k = k * lax.rsqrt(_split_dot(k * k, bdk_ref) + EPS) * kg_ref[...]
    v = proj(WA_Q + WA_KV, WA_KV)
    lower = lax.broadcasted_iota(I32, (1, WA_KV), 1) < HEAD_DIM_A
    for t, ref in ((k, k_ref), (v, v_ref)):
        head0_lo = jnp.where(lower, t, 0.0)
        head1_hi = jnp.where(lower, 0.0, t)
        variants = (head0_lo, pltpu.roll(head0_lo, HEAD_DIM_A, 1), pltpu.roll(head1_hi, HEAD_DIM_A, 1), head1_hi)
        for i, var in enumerate(variants):
            ref[0, :, i * WA_KV:(i + 1) * WA_KV] = var.astype(BF16)

    cc = cc_ref[...]
    ss = ss_ref[...]
    base = WA_Q + 2 * WA_KV
    for off, ref, scale in ((base, rq_ref, 1.0), (base + WR, rk_ref, HEAD_DIM_R ** -0.5)):
        r = proj(off, WR)
        for hh in range(N_HEADS_R):
            seg = r[:, hh * HEAD_DIM_R:(hh + 1) * HEAD_DIM_R]
            rot = seg * cc + pltpu.roll(seg, HEAD_DIM_R // 2, 1) * ss
            ref[0, :, hh * HEAD_DIM_R:(hh + 1) * HEAD_DIM_R] = (rot * scale).astype(BF16)
    rv_ref[0] = proj(base + 2 * WR, WR).astype(BF16)
    gf_ref[0] = proj(base + 3 * WR, WR).astype(BF16)
    gb_ref[0] = proj(base + 4 * WR, WR).astype(BF16)


def _in_proj(hp, g1, w_in_bf, cc, ss, qg, kg, bdq, bdk):
    B, Lp, D = hp.shape
    tm = _row_tile(Lp)
    const = lambda b, j: (0, 0)
    row = lambda n: pl.BlockSpec((1, tm, n), lambda b, j: (b, j, 0))
    widths = (WA_Q, 2 * N_KV_A * WA_KV, 2 * N_KV_A * WA_KV, WR, WR, WR, WR, WR)
    return pl.pallas_call(
        functools.partial(_in_proj_kernel, tm=tm),
        grid=(B, Lp // tm),
        in_specs=[row(D),
                  pl.BlockSpec((1, D), const),
                  pl.BlockSpec((D, D_IN), const),
                  pl.BlockSpec((tm, HEAD_DIM_R), lambda b, j: (j, 0)),
                  pl.BlockSpec((tm, HEAD_DIM_R), lambda b, j: (j, 0)),
                  pl.BlockSpec((1, WA_Q), const),
                  pl.BlockSpec((1, WA_KV), const),
                  pl.BlockSpec((WA_Q, WA_Q), const),
                  pl.BlockSpec((WA_KV, WA_KV), const)],
        out_specs=[row(n) for n in widths],
        out_shape=[jax.ShapeDtypeStruct((B, Lp, n), BF16) for n in widths],
        compiler_params=_params(("parallel", "parallel"), 48 << 20),
        name="in_proj",
    )(hp, g1, w_in_bf, cc, ss, qg, kg, bdq, bdk)


def _ret_state_kernel(lg_ref, kf_ref, vf_ref, kb_ref, vb_ref, sf_ref, sb_ref, st_ref, *, nsub):
    t = pl.program_id(0)

    @pl.when(t == 0)
    def _():
        st_ref[...] = jnp.zeros_like(st_ref)

    idx = lax.broadcasted_iota(I32, (BLOCK, 1), 0).astype(F32)
    ones_row = jnp.ones((1, HEAD_DIM_R), F32)
    for d, (k_ref, v_ref, s_out) in enumerate(((kf_ref, vf_ref, sf_ref), (kb_ref, vb_ref, sb_ref))):
        for h in range(N_HEADS_R):
            lg = lg_ref[d, h]
            sl = slice(h * HEAD_DIM_R, (h + 1) * HEAD_DIM_R)
            zeta = jnp.exp(lg * ((BLOCK - 1.0 - idx) if d == 0 else idx))
            chunk_decay = jnp.exp(lg * float(BLOCK) * ones_row)
            for b in range(st_ref.shape[0]):
                st = st_ref[b, d, h]
                for step in range(nsub):
                    j = step if d == 0 else nsub - 1 - step
                    rows = slice(j * BLOCK, (j + 1) * BLOCK)
                    s_out[b, j, h] = st.astype(BF16)
                    kz = (k_ref[b, rows, sl].astype(F32) * zeta).astype(BF16)
                    kv = lax.dot_general(kz, v_ref[b, rows, sl], (((0,), (0,)), ((), ())),
                                         preferred_element_type=F32)
                    st = st * chunk_decay + kv
                st_ref[b, d, h] = st


def _ret_states(lg2, rk, rv):
    B, Lp, _ = rk.shape
    tm = _row_tile(Lp)
    nsub = tm // BLOCK
    nt = Lp // tm
    fwd = pl.BlockSpec((B, tm, WR), lambda t: (0, t, 0))
    bwd = pl.BlockSpec((B, tm, WR), lambda t: (0, nt - 1 - t, 0))
    st_shape = (B, nsub, N_HEADS_R, HEAD_DIM_R, HEAD_DIM_R)
    return pl.pallas_call(
        functools.partial(_ret_state_kernel, nsub=nsub),
        grid=(nt,),
        in_specs=[pl.BlockSpec(memory_space=pltpu.SMEM), fwd, fwd, bwd, bwd],
        out_specs=[pl.BlockSpec(st_shape, lambda t: (0, t, 0, 0, 0)),
                   pl.BlockSpec(st_shape, lambda t: (0, nt - 1 - t, 0, 0, 0))],
        out_shape=[jax.ShapeDtypeStruct((B, Lp // BLOCK) + st_shape[2:], BF16)] * 2,
        scratch_shapes=[pltpu.VMEM((B, 2, N_HEADS_R, HEAD_DIM_R, HEAD_DIM_R), F32)],
        compiler_params=_params(("arbitrary",), 48 << 20),
        name="ret_state",
    )(lg2, rk, rv, rk, rv)


def _mixer_kernel(lg_ref, sink_ref, q_ref, kp_ref, kc_ref, kn_ref, vp_ref, vc_ref, vn_ref,
                  rq_ref, rk_ref, rv_ref, gf_ref, gb_ref, sf_ref, sb_ref, bias_ref,
                  h_ref, w_ref, g2_ref, wr_ref, h1_ref, hn_ref, aff_ref,
                  dec_ref, xi_ref, kbuf_ref, vbuf_ref, mix_ref, *, nblocks, tm):
    b = pl.program_id(0)
    c = pl.program_id(1)
    nsub = tm // BLOCK
    kv_w = 3 * BLOCK

    @pl.when((b == 0) & (c == 0))
    def _():
        ii = lax.broadcasted_iota(I32, (BLOCK, BLOCK), 0).astype(F32)
        jj = lax.broadcasted_iota(I32, (BLOCK, BLOCK), 1).astype(F32)
        for d in range(2):
            diff = (ii - jj) if d == 0 else (jj - ii)
            reach = (ii + 1.0) if d == 0 else (float(BLOCK) - ii)
            for h in range(N_HEADS_R):
                lg = lg_ref[d, h]
                dec_ref[d, h] = jnp.where(diff >= 0, jnp.exp(lg * jnp.maximum(diff, 0.0)), 0.0)
                xi_ref[d, h] = jnp.exp(lg * reach)

    for buf, (p_ref, c_ref, n_ref) in ((kbuf_ref, (kp_ref, kc_ref, kn_ref)), (vbuf_ref, (vp_ref, vc_ref, vn_ref))):
        buf[0:BLOCK] = p_ref[0]
        buf[BLOCK:BLOCK + tm] = c_ref[0]
        buf[BLOCK + tm:] = n_ref[0]

    lower_lanes = lax.broadcasted_iota(I32, (1, 2 * HEAD_DIM_A), 1) < HEAD_DIM_A
    upper_rows = lax.broadcasted_iota(I32, (2 * BLOCK, 1), 0) >= BLOCK

    def block(i, carry):
        r0 = pl.multiple_of(i * BLOCK, BLOCK)
        rows = pl.ds(r0, BLOCK)
        win = pl.ds(r0, kv_w)

        n = c * nsub + i
        variant = jnp.where(n < 2, n, jnp.where(n == nblocks - 1, 3, 2))
        for kh in range(N_KV_A):
            g0 = slice(2 * kh * WA_KV, (2 * kh + 1) * WA_KV)
            g1 = slice((2 * kh + 1) * WA_KV, (2 * kh + 2) * WA_KV)
            q2 = jnp.concatenate([q_ref[0, rows, g0], q_ref[0, rows, g1]], axis=0)
            kcat = jnp.concatenate([kbuf_ref[win, g0], kbuf_ref[win, g1]], axis=0)
            vcat = jnp.concatenate([vbuf_ref[win, g0], vbuf_ref[win, g1]], axis=0)
            s = lax.dot_general(q2, kcat, (((1,), (1,)), ((), ())), preferred_element_type=F32)
            s = s + bias_ref[variant, kh]
            probs = []
            denoms = []
            for half in range(2):
                sh = s[:, half * kv_w:(half + 1) * kv_w]
                sk = jnp.where(upper_rows, sink_ref[4 * kh + 2 + half], sink_ref[4 * kh + half])
                m = jnp.maximum(jnp.max(sh, axis=-1, keepdims=True), sk)
                p = jnp.exp2(sh - m)
                denoms.append(jnp.sum(p, axis=-1, keepdims=True) + jnp.exp2(sk - m))
                probs.append(p.astype(BF16))
            o = jnp.dot(jnp.concatenate(probs, axis=1), vcat, preferred_element_type=F32)
            o = o / jnp.where(lower_lanes, denoms[0], denoms[1])
            mix_ref[rows, g0] = o[:BLOCK].astype(BF16)
            mix_ref[rows, g1] = o[BLOCK:].astype(BF16)

        return carry

    def ret_block(i, carry):
        r0 = pl.multiple_of(i * BLOCK, BLOCK)
        rows = pl.ds(r0, BLOCK)
        for h in range(N_HEADS_R):
            sl = slice(h * HEAD_DIM_R, (h + 1) * HEAD_DIM_R)
            qh = rq_ref[0, rows, sl]
            vh = rv_ref[0, rows, sl]
            qk = lax.dot_general(qh, rk_ref[0, rows, sl], (((1,), (1,)), ((), ())),
                                 preferred_element_type=F32)
            qf = qh.astype(F32)
            acc = None
            for d, (st_ref, g_ref) in enumerate(((sf_ref, gf_ref), (sb_ref, gb_ref))):
                lhs = jnp.concatenate([(qk * dec_ref[d, h]).astype(BF16), (qf * xi_ref[d, h]).astype(BF16)], axis=1)
                rhs = jnp.concatenate([vh, st_ref[0, i, h]], axis=0)
                y = jnp.dot(lhs, rhs, preferred_element_type=F32)
                mu = jnp.mean(y, axis=-1, keepdims=True)
                yc = y - mu
                var = jnp.mean(yc * yc, axis=-1, keepdims=True)
                gate = g_ref[0, rows, sl].astype(F32)
                term = gate * jax.nn.sigmoid(gate) * (yc * lax.rsqrt(var + EPS))
                acc = term if acc is None else acc + term
            mix_ref[rows, WA_Q + h * HEAD_DIM_R:WA_Q + (h + 1) * HEAD_DIM_R] = acc.astype(BF16)
        return carry

    for i in range(nsub):
        block(i, 0)
    for i in range(nsub):
        ret_block(i, 0)

    h1 = h_ref[0] + jnp.dot(mix_ref[...], w_ref[...], preferred_element_type=F32)
    h1_ref[0] = h1
    ms = jnp.mean(h1 * h1, axis=-1, keepdims=True)
    xn = h1 * lax.rsqrt(ms + EPS) * g2_ref[...]
    hn_ref[0] = xn.astype(BF16)
    hi = xn.astype(BF16)
    lo = (xn - hi.astype(F32)).astype(BF16)
    both = jnp.dot(hi, wr_ref[...], preferred_element_type=F32)
    logits = (both[:, :LANES] + both[:, LANES:]
              + jnp.dot(lo, wr_ref[:, :LANES], preferred_element_type=F32))
    lt = logits.T[:N_EXPERTS]
    mx = jnp.max(lt, axis=0, keepdims=True)
    ex = jnp.exp(lt - mx)
    aff = ex / jnp.sum(ex, axis=0, keepdims=True)
    col = c * tm + lax.broadcasted_iota(I32, (1, tm), 1)
    aff_ref[0] = jnp.where(col >= PAD_FRONT, aff, -1.0)


def _mixer(lg2, sink, q, k4, v4, rq, rk, rv, gf, gb, sf, sb, bias2, hp, w_out_bf, g2, wr_split):
    B, Lp, D = hp.shape
    nb = Lp // BLOCK
    assert nb >= 3
    tm = _row_tile(Lp)
    nsub = tm // BLOCK
    kvw = k4.shape[-1]
    const = lambda b, c: (0, 0)
    cur = lambda n_: pl.BlockSpec((1, tm, n_), lambda b, c: (b, c, 0))
    prev = pl.BlockSpec((1, BLOCK, kvw), lambda b, c: (b, jnp.maximum(c * nsub - 1, 0), 0))
    nxt = pl.BlockSpec((1, BLOCK, kvw), lambda b, c: (b, jnp.minimum((c + 1) * nsub, nb - 1), 0))
    st = pl.BlockSpec((1, nsub, N_HEADS_R, HEAD_DIM_R, HEAD_DIM_R), lambda b, c: (b, c, 0, 0, 0))
    smem = pl.BlockSpec(memory_space=pltpu.SMEM)
    tab = pltpu.VMEM((2, N_HEADS_R, BLOCK, HEAD_DIM_R), F32)
    kvbuf = pltpu.VMEM((tm + 2 * BLOCK, kvw), BF16)
    return pl.pallas_call(
        functools.partial(_mixer_kernel, nblocks=nb, tm=tm),
        grid=(B, Lp // tm),
        in_specs=[smem, smem, cur(WA_Q), prev, cur(kvw), nxt, prev, cur(kvw), nxt,
                  cur(WR), cur(WR), cur(WR), cur(WR), cur(WR), st, st,
                  pl.BlockSpec(bias2.shape, lambda b, c: (0, 0, 0, 0)),
                  cur(D), pl.BlockSpec((D, D), const), pl.BlockSpec((1, D), const),
                  pl.BlockSpec((D, 2 * LANES), const)],
        out_specs=[cur(D), cur(D), pl.BlockSpec((1, N_EXPERTS, tm), lambda b, c: (b, 0, c))],
        out_shape=[jax.ShapeDtypeStruct((B, Lp, D), F32),
                   jax.ShapeDtypeStruct((B, Lp, D), BF16),
                   jax.ShapeDtypeStruct((B, N_EXPERTS, Lp), F32)],
        scratch_shapes=[tab, tab, kvbuf, kvbuf, pltpu.VMEM((tm, D), BF16)],
        compiler_params=_params(("arbitrary", "arbitrary"), 56 << 20),
        name="mixer",
    )(lg2, sink, q, k4, k4, k4, v4, v4, v4, rq, rk, rv, gf, gb, sf, sb, bias2, hp, w_out_bf, g2, wr_split)


def _route_kernel(aff_ref, posm_ref, cpos_ref, *, cap, nb):
    E = aff_ref.shape[0]

    def keys(sl=slice(None)):
        return pltpu.bitcast(aff_ref[:, sl], I32)

    def count_ge(t):
        return jnp.sum((keys() >= t).astype(I32), axis=-1, keepdims=True)

    def search(_, carry):
        lo, hi = carry
        mid = lo + (hi - lo) // 2
        ok = count_ge(mid) >= cap
        return jnp.where(ok, mid, lo), jnp.where(ok, hi, mid)

    lo0 = jnp.zeros((E, 1), I32)
    hi0 = jnp.full((E, 1), 0x3F800001, I32)
    thr, _ = lax.fori_loop(0, 31, search, (lo0, hi0))
    n_gt = jnp.sum((keys() > thr).astype(F32), axis=-1, keepdims=True)
    need_eq = float(cap) - n_gt

    jr = lax.broadcasted_iota(I32, (LANES, LANES), 0)
    jc = lax.broadcasted_iota(I32, (LANES, LANES), 1)
    upper = (jr < jc).astype(F32).astype(BF16)
    lane = lax.broadcasted_iota(I32, (E, LANES), 1)

    def chunk(c, carry):
        cg, ce, cp = carry
        off = pl.multiple_of(c * LANES, LANES)
        kk = pltpu.bitcast(aff_ref[:, pl.ds(off, LANES)], I32)
        gt = kk > thr
        eq = kk == thr
        both = jnp.concatenate([gt, eq], axis=0).astype(F32)
        ex = jnp.dot(both.astype(BF16), upper, preferred_element_type=F32)
        exg = ex[:E] + cg
        exe = ex[E:] + ce
        sel = gt | (eq & (exe < need_eq))
        pos = exg + jnp.minimum(exe, need_eq)
        posm_ref[:, pl.ds(off, LANES)] = jnp.where(sel, pos, -1.0).astype(I32)
        cp = jnp.where(lane == c, (cg + jnp.minimum(ce, need_eq)).astype(I32), cp)
        tot = jnp.sum(both, axis=-1, keepdims=True)
        return cg + tot[:E], ce + tot[E:], cp

    zero = jnp.zeros((E, 1), F32)
    cg, ce, cp = lax.fori_loop(0, nb, chunk, (zero, zero, jnp.zeros((E, LANES), I32)))
    cpos_ref[...] = jnp.where(lane == nb, (cg + jnp.minimum(ce, need_eq)).astype(I32), cp)


def _route(aff_t, cap):
    B, E, Lp = aff_t.shape
    nb = Lp // LANES
    assert nb + 1 <= LANES
    posm, cpos = pl.pallas_call(
        functools.partial(_route_kernel, cap=cap, nb=nb),
        grid=(1,),
        in_specs=[pl.BlockSpec((B * E, Lp), lambda i: (0, 0))],
        out_specs=[pl.BlockSpec((B * E, Lp), lambda i: (0, 0)),
                   pl.BlockSpec((B * E, LANES), lambda i: (0, 0))],
        out_shape=[jax.ShapeDtypeStruct((B * E, Lp), I32),
                   jax.ShapeDtypeStruct((B * E, LANES), I32)],
        compiler_params=_params(("arbitrary",), 32 << 20),
        name="route",
    )(aff_t.reshape(B * E, Lp))
    return posm.reshape(B, E, Lp), cpos


NARROW_WINDOW = 3 * BF16_SUBLANES
WIDE_WINDOW = BLOCK + BF16_SUBLANES


def _window(cpos_ref, b, e, c, w, cpad):
    base = (b * N_EXPERTS + e) * LANES + c
    a0 = jnp.minimum((cpos_ref[base] // BF16_SUBLANES) * BF16_SUBLANES, cpad - w)
    return pl.multiple_of(a0, BF16_SUBLANES), cpos_ref[base + 1] - a0 <= w


def _windows(cpos_ref, b, experts, c, w, cpad):
    starts = []
    fits = None
    for e in experts:
        a0, ok = _window(cpos_ref, b, e, c, w, cpad)
        starts.append(a0)
        fits = ok if fits is None else fits & ok
    return starts, fits


EXPERT_GROUP = 8


def _gather_kernel(cpos_ref, hn_ref, posm_ref, x_ref, *, nsub, cpad):
    b = pl.program_id(0)
    g = pl.program_id(1)
    co = pl.program_id(2)
    experts = [g * EXPERT_GROUP + e for e in range(EXPERT_GROUP)]

    @pl.when(co == 0)
    def _():
        x_ref[...] = jnp.zeros_like(x_ref)

    def sub_chunk(i, carry):
        c = co * nsub + i
        off = pl.multiple_of(i * BLOCK, BLOCK)

        narrow_starts, narrow = _windows(cpos_ref, b, experts, c, NARROW_WINDOW, cpad)

        def move(w):
            rows = lax.broadcasted_iota(I32, (w, BLOCK), 0)
            starts = narrow_starts if w == NARROW_WINDOW else _windows(cpos_ref, b, experts, c, w, cpad)[0]
            onehots = []
            for e in range(EXPERT_GROUP):
                hit = rows + starts[e] == posm_ref[0, 0, e:e + 1, pl.ds(off, BLOCK)]
                onehots.append(hit.astype(F32).astype(BF16))
            res = jnp.dot(jnp.concatenate(onehots, axis=0), hn_ref[0, pl.ds(off, BLOCK), :],
                          preferred_element_type=F32).astype(BF16)
            for e in range(EXPERT_GROUP):
                x_ref[0, e, pl.ds(starts[e], w), :] += res[e * w:(e + 1) * w]

        pl.when(narrow)(functools.partial(move, NARROW_WINDOW))
        pl.when(jnp.logical_not(narrow))(functools.partial(move, WIDE_WINDOW))
        return carry

    lax.fori_loop(0, nsub, sub_chunk, 0)


def _gather(cpos_flat, hn, posm, cpad):
    B, Lp, D = hn.shape
    E = N_EXPERTS
    ng = E // EXPERT_GROUP
    tm = _row_tile(Lp)
    posm4 = posm.reshape(B, ng, EXPERT_GROUP, Lp)
    grid_spec = pltpu.PrefetchScalarGridSpec(
        num_scalar_prefetch=1,
        grid=(B, ng, Lp // tm),
        in_specs=[pl.BlockSpec((1, tm, D), lambda b, g, c, cp: (b, c, 0)),
                  pl.BlockSpec((1, 1, EXPERT_GROUP, tm), lambda b, g, c, cp: (b, g, 0, c))],
        out_specs=pl.BlockSpec((1, EXPERT_GROUP, cpad, D), lambda b, g, c, cp: (b, g, 0, 0)),
    )
    return pl.pallas_call(
        functools.partial(_gather_kernel, nsub=tm // BLOCK, cpad=cpad),
        grid_spec=grid_spec,
        out_shape=jax.ShapeDtypeStruct((B, E, cpad, D), BF16),
        compiler_params=_params(("arbitrary", "arbitrary", "arbitrary"), 48 << 20),
        name="gather",
    )(cpos_flat, hn, posm4)


FFN_TILE = 256
FFN_BATCH = 2


def _ffn_kernel(x_ref, wg_ref, wu_ref, wd_ref, y_ref, hm_ref, wd_bf_ref, *, tf):
    f = pl.program_id(2)
    bh, _, cpad, _ = x_ref.shape
    x = x_ref[...].reshape(bh * cpad, D_MODEL)
    a = jnp.dot(x, wg_ref[0, 0].astype(BF16), preferred_element_type=F32)
    u = jnp.dot(x, wu_ref[0, 0].astype(BF16), preferred_element_type=F32)
    off = pl.multiple_of(f * tf, tf)
    hm_ref[:, pl.ds(off, tf)] = (a * jax.nn.sigmoid(a) * u).astype(BF16)
    wd_bf_ref[pl.ds(off, tf), :] = wd_ref[0, 0].astype(BF16)

    @pl.when(f == pl.num_programs(2) - 1)
    def _():
        for i in range(bh):
            y_ref[i, 0] = jnp.dot(hm_ref[i * cpad:(i + 1) * cpad, :], wd_bf_ref[...],
                                  preferred_element_type=F32).astype(BF16)


def _ffn(xe, w_gate, w_up, w_down, layer):
    B, E, cpad, D = xe.shape
    F = w_gate.shape[-1]
    tf = FFN_TILE if F % FFN_TILE == 0 else LANES
    bh = FFN_BATCH if B % FFN_BATCH == 0 else 1
    xspec = pl.BlockSpec((bh, 1, cpad, D), lambda e, b, f: (b, e, 0, 0))
    return pl.pallas_call(
        functools.partial(_ffn_kernel, tf=tf),
        grid=(E, B // bh, F // tf),
        in_specs=[xspec,
                  pl.BlockSpec((1, 1, D, tf), lambda e, b, f: (layer, e, 0, f)),
                  pl.BlockSpec((1, 1, D, tf), lambda e, b, f: (layer, e, 0, f)),
                  pl.BlockSpec((1, 1, tf, D), lambda e, b, f: (layer, e, f, 0))],
        out_specs=xspec,
        out_shape=jax.ShapeDtypeStruct(xe.shape, BF16),
        scratch_shapes=[pltpu.VMEM((bh * cpad, F), BF16), pltpu.VMEM((F, D), BF16)],
        compiler_params=_params(("parallel", "parallel", "arbitrary"), 56 << 20),
        name="ffn",
    )(xe, w_gate, w_up, w_down)


def _combine_kernel(cpos_ref, h_ref, posm_ref, gate_ref, ye_ref, o_ref, *, nsub, chunk0, cpad):
    b = pl.program_id(0)
    co = pl.program_id(1)
    experts = list(range(N_EXPERTS))

    def move(c, tok0, ntok, w, starts=None):
        tokens = pl.ds(tok0, ntok)
        if starts is None:
            starts = _windows(cpos_ref, b, experts, c, w, cpad)[0]
        rows = lax.broadcasted_iota(I32, (w, ntok), 0)
        weights = []
        windows = []
        for e in experts:
            hit = rows + starts[e] == posm_ref[0, e:e + 1, tokens]
            weights.append(jnp.where(hit, gate_ref[0, e:e + 1, tokens], 0.0).astype(BF16))
            windows.append(ye_ref[0, e, pl.ds(starts[e], w), :])
        moe = lax.dot_general(jnp.concatenate(weights, axis=0), jnp.concatenate(windows, axis=0),
                              (((0,), (0,)), ((), ())), preferred_element_type=F32)
        o_ref[0, tokens, :] = h_ref[0, tokens, :] + moe

    def sub_chunk(i, carry):
        c = chunk0 + co * nsub + i
        tok0 = pl.multiple_of(i * BLOCK, BLOCK)
        narrow_starts, narrow = _windows(cpos_ref, b, experts, c, NARROW_WINDOW, cpad)
        pl.when(narrow)(functools.partial(move, c, tok0, BLOCK, NARROW_WINDOW, narrow_starts))
        pl.when(jnp.logical_not(narrow))(functools.partial(move, c, tok0, BLOCK, WIDE_WINDOW))
        return carry

    lax.fori_loop(0, nsub, sub_chunk, 0)


def _combine(cpos_flat, h1, posm, gate, ye, cpad, skip_rows=0):
    B, Lp, D = h1.shape
    E = N_EXPERTS
    rows_out = Lp - skip_rows
    tm = _row_tile(rows_out)
    assert skip_rows % BLOCK == 0
    el = lambda *dims: tuple(pl.Element(d) for d in dims)
    first = lambda c: pl.multiple_of(skip_rows + c * tm, BLOCK)
    grid_spec = pltpu.PrefetchScalarGridSpec(
        num_scalar_prefetch=1,
        grid=(B, rows_out // tm),
        in_specs=[pl.BlockSpec(el(1, tm, D), lambda b, c, cp: (b, first(c), 0)),
                  pl.BlockSpec(el(1, E, tm), lambda b, c, cp: (b, 0, first(c))),
                  pl.BlockSpec(el(1, E, tm), lambda b, c, cp: (b, 0, first(c))),
                  pl.BlockSpec((1, E, cpad, D), lambda b, c, cp: (b, 0, 0, 0),
                               pipeline_mode=pl.Buffered(1))],
        out_specs=pl.BlockSpec((1, tm, D), lambda b, c, cp: (b, c, 0)),
    )
    return pl.pallas_call(
        functools.partial(_combine_kernel, nsub=tm // BLOCK, chunk0=skip_rows // BLOCK, cpad=cpad),
        grid_spec=grid_spec,
        out_shape=jax.ShapeDtypeStruct((B, rows_out, D), F32),
        compiler_params=_params(("arbitrary", "arbitrary"), VMEM_LIMIT_CAP),
        name="combine",
    )(cpos_flat, h1, posm, gate, ye)


def _t5_bucket(rel):
    half = N_BUCKETS // 2
    max_exact = half // 2
    n = jnp.abs(rel)
    large = max_exact + (jnp.log(jnp.maximum(n, max_exact).astype(F32) / max_exact)
                         / math.log(MAX_DISTANCE / max_exact) * (half - max_exact)).astype(I32)
    large = jnp.minimum(large, half - 1)
    return jnp.where(rel > 0, half, 0) + jnp.where(n < max_exact, n, large)


def _block_diag_mean(n, group):
    i = jnp.arange(n)
    return jnp.where((i[:, None] // group) == (i[None, :] // group), 1.0 / group, 0.0).astype(BF16)


def kernel(x, meta_tokens, rel_bias, norm1_g, w_in, q_norm_g, k_norm_g, attn_sink, ret_decay,
           w_out, norm2_g, w_router, w_gate, w_up, w_down):
    B, seq, D = x.shape
    depth = w_in.shape[0]
    L = seq + N_META
    Lp = L + PAD_FRONT
    assert D == D_MODEL and Lp % BLOCK == 0
    cap = CAPACITY_FACTOR * L // N_EXPERTS
    cpad = -(-max(cap, WIDE_WINDOW) // BF16_SUBLANES) * BF16_SUBLANES

    hp = jnp.concatenate([jnp.zeros((B, PAD_FRONT, D), x.dtype),
                          jnp.broadcast_to(meta_tokens.astype(x.dtype)[None], (B, N_META, D)), x], axis=1)

    pos = jnp.arange(Lp, dtype=F32)
    inv = ROPE_BASE ** (-jnp.arange(0, HEAD_DIM_R, 2, dtype=F32) / HEAD_DIM_R)
    ang = pos[:, None] * inv[None]
    cc = jnp.concatenate([jnp.cos(ang), jnp.cos(ang)], axis=-1)
    ss = jnp.concatenate([-jnp.sin(ang), jnp.sin(ang)], axis=-1)
    rel = (jnp.arange(3 * BLOCK)[None, :] - BLOCK) - jnp.arange(BLOCK)[:, None]
    bucket_onehot = (_t5_bucket(rel)[None] == jnp.arange(N_BUCKETS)[:, None, None]).astype(F32)
    bias = jnp.einsum('kqs,kh->hqs', bucket_onehot, rel_bias.astype(F32), precision=lax.Precision.HIGHEST)
    bias = jnp.where((jnp.abs(rel) <= BLOCK)[None], bias * LOG2E, NEG)
    key = jnp.arange(3 * BLOCK)
    bias = jnp.stack([jnp.where(key >= BLOCK + PAD_FRONT, bias, NEG), jnp.where(key >= PAD_FRONT, bias, NEG),
                      bias, jnp.where(key < 2 * BLOCK, bias, NEG)])
    bias = bias.reshape(4, N_KV_A, 2, 2, BLOCK, 3 * BLOCK).transpose(0, 1, 2, 4, 3, 5)
    bias = bias.reshape(4, N_KV_A, 2 * BLOCK, 6 * BLOCK)
    bdq = _block_diag_mean(WA_Q, HEAD_DIM_A)
    bdk = _block_diag_mean(WA_KV, HEAD_DIM_A)

    for l in range(depth):
        qg = (jnp.tile(q_norm_g[l].astype(F32), N_HEADS_A) * (HEAD_DIM_A ** -0.5 * LOG2E))[None]
        kg = jnp.tile(k_norm_g[l].astype(F32), N_KV_A)[None]
        lg2 = -jnp.exp(ret_decay[l].astype(F32))
        q, k, v, rq, rk, rv, gf, gb = _in_proj(hp, norm1_g[l][None].astype(F32), w_in[l].astype(BF16),
                                               cc, ss, qg, kg, bdq, bdk)
        sf, sb = _ret_states(lg2, rk, rv)
        wr = jnp.pad(w_router[l].astype(F32), ((0, 0), (0, LANES - N_EXPERTS)))
        wr_hi = wr.astype(BF16)
        wr_lo = (wr - wr_hi.astype(F32)).astype(BF16)
        wr_split = jnp.concatenate([wr_hi, wr_lo], axis=1)
        h1, hn, aff = _mixer(lg2, attn_sink[l].astype(F32) * LOG2E, q, k, v, rq, rk, rv, gf, gb, sf, sb, bias,
                             hp, w_out[l].astype(BF16), norm2_g[l][None].astype(F32), wr_split)
        posm, cpos = _route(aff, cap)
        cpos_flat = cpos.reshape(-1)
        xe = _gather(cpos_flat, hn, posm, cpad)
        ye = _ffn(xe, w_gate, w_up, w_down, l)
        last = l == depth - 1
        hp = _combine(cpos_flat, h1, posm, aff, ye, cpad, skip_rows=(PAD_FRONT + N_META) if last else 0)
    return hp
```

```python
import functools
import math

import jax
import jax.numpy as jnp
from jax import lax
from jax.experimental import pallas as pl
from jax.experimental.pallas import tpu as pltpu

D_MODEL = 1024
N_META = 16
BLOCK = 128
PAD_FRONT = BLOCK - N_META
EPS = 1e-6
N_HEADS_A = 8
N_KV_A = 2
GROUP_A = N_HEADS_A // N_KV_A
HEAD_DIM_A = 64
N_HEADS_R = 4
HEAD_DIM_R = 128
ROPE_BASE = 10000.0
N_BUCKETS = 32
MAX_DISTANCE = 128
N_EXPERTS = 16
CAPACITY_FACTOR = 2
WA_Q = N_HEADS_A * HEAD_DIM_A
WA_KV = N_KV_A * HEAD_DIM_A
WR = N_HEADS_R * HEAD_DIM_R
D_IN = WA_Q + 2 * WA_KV + 5 * WR
NEG = -1e30
LOG2E = math.log2(math.e)

LANES = 128
BF16_SUBLANES = 16
VMEM_LIMIT_CAP = 60000 * 1024

F32 = jnp.float32
BF16 = jnp.bfloat16
I32 = jnp.int32


def _params(semantics, vmem_bytes, **extra):
    return pltpu.CompilerParams(dimension_semantics=semantics,
                                vmem_limit_bytes=min(int(vmem_bytes), VMEM_LIMIT_CAP), **extra)


def _row_tile(lp):
    best = BLOCK
    for t in range(BLOCK, 1024 + 1, BLOCK):
        if lp % t == 0:
            best = t
    return best


def _split_dot(x, w_ref):
    hi = x.astype(BF16)
    lo = (x - hi.astype(F32)).astype(BF16)
    w = w_ref[...]
    return (jnp.dot(hi, w, preferred_element_type=F32) + jnp.dot(lo, w, preferred_element_type=F32))


def _in_proj_kernel(h_ref, g1_ref, w_ref, cc_ref, ss_ref, qg_ref, kg_ref, bdq_ref, bdk_ref,
                    q_ref, k_ref, v_ref, rq_ref, rk_ref, rv_ref, gf_ref, gb_ref, *, tm):
    j = pl.program_id(1)
    x = h_ref[0]
    ms = jnp.mean(x * x, axis=-1, keepdims=True)
    xn = x * lax.rsqrt(ms + EPS) * g1_ref[...]
    row = j * tm + lax.broadcasted_iota(I32, (tm, 1), 0)
    xn = jnp.where(row >= PAD_FRONT, xn, 0.0).astype(BF16)

    def proj(lo, n):
        return jnp.dot(xn, w_ref[:, lo:lo + n], preferred_element_type=F32)

    q = proj(0, WA_Q)
    q_ms = jnp.dot((q * q).astype(BF16), bdq_ref[...], preferred_element_type=F32)
    q_ref[0] = (q * lax.rsqrt(q_ms + EPS) * qg_ref[...]).astype(BF16)
    k = proj(WA_Q, WA_KV)
    k = k * lax.rsqrt(_split_dot(k * k, bdk_ref) + EPS) * kg_ref[...]
    v = proj(WA_Q + WA_KV, WA_KV)
    lower = lax.broadcasted_iota(I32, (1, WA_KV), 1) < HEAD_DIM_A
    for t, ref in ((k, k_ref), (v, v_ref)):
        head0_lo = jnp.where(lower, t, 0.0)
        head1_hi = jnp.where(lower, 0.0, t)
        variants = (head0_lo, pltpu.roll(head0_lo, HEAD_DIM_A, 1), pltpu.roll(head1_hi, HEAD_DIM_A, 1), head1_hi)
        for i, var in enumerate(variants):
            ref[0, :, i * WA_KV:(i + 1) * WA_KV] = var.astype(BF16)

    cc = cc_ref[...]
    ss = ss_ref[...]
    base = WA_Q + 2 * WA_KV
    for off, ref, scale in ((base, rq_ref, 1.0), (base + WR, rk_ref, HEAD_DIM_R ** -0.5)):
        r = proj(off, WR)
        for hh in range(N_HEADS_R):
            seg = r[:, hh * HEAD_DIM_R:(hh + 1) * HEAD_DIM_R]
            rot = seg * cc + pltpu.roll(seg, HEAD_DIM_R // 2, 1) * ss
            ref[0, :, hh * HEAD_DIM_R:(hh + 1) * HEAD_DIM_R] = (rot * scale).astype(BF16)
    rv_ref[0] = proj(base + 2 * WR, WR).astype(BF16)
    gf_ref[0] = proj(base + 3 * WR, WR).astype(BF16)
    gb_ref[0] = proj(base + 4 * WR, WR).astype(BF16)


def _in_proj(hp, g1, w_in_bf, cc, ss, qg, kg, bdq, bdk):
    B, Lp, D = hp.shape
    tm = _row_tile(Lp)
    const = lambda b, j: (0, 0)
    row = lambda n: pl.BlockSpec((1, tm, n), lambda b, j: (b, j, 0))
    widths = (WA_Q, 2 * N_KV_A * WA_KV, 2 * N_KV_A * WA_KV, WR, WR, WR, WR, WR)
    return pl.pallas_call(
        functools.partial(_in_proj_kernel, tm=tm),
        grid=(B, Lp // tm),
        in_specs=[row(D),
                  pl.BlockSpec((1, D), const),
                  pl.BlockSpec((D, D_IN), const),
                  pl.BlockSpec((tm, HEAD_DIM_R), lambda b, j: (j, 0)),
                  pl.BlockSpec((tm, HEAD_DIM_R), lambda b, j: (j, 0)),
                  pl.BlockSpec((1, WA_Q), const),
                  pl.BlockSpec((1, WA_KV), const),
                  pl.BlockSpec((WA_Q, WA_Q), const),
                  pl.BlockSpec((WA_KV, WA_KV), const)],
        out_specs=[row(n) for n in widths],
        out_shape=[jax.ShapeDtypeStruct((B, Lp, n), BF16) for n in widths],
        compiler_params=_params(("parallel", "parallel"), 48 << 20),
        name="in_proj",
    )(hp, g1, w_in_bf, cc, ss, qg, kg, bdq, bdk)


def _ret_state_kernel(lg_ref, kf_ref, vf_ref, kb_ref, vb_ref, sf_ref, sb_ref, st_ref, *, nsub):
    t = pl.program_id(0)

    @pl.when(t == 0)
    def _():
        st_ref[...] = jnp.zeros_like(st_ref)

    idx = lax.broadcasted_iota(I32, (BLOCK, 1), 0).astype(F32)
    ones_row = jnp.ones((1, HEAD_DIM_R), F32)
    for d, (k_ref, v_ref, s_out) in enumerate(((kf_ref, vf_ref, sf_ref), (kb_ref, vb_ref, sb_ref))):
        for h in range(N_HEADS_R):
            lg = lg_ref[d, h]
            sl = slice(h * HEAD_DIM_R, (h + 1) * HEAD_DIM_R)
            zeta = jnp.exp(lg * ((BLOCK - 1.0 - idx) if d == 0 else idx))
            chunk_decay = jnp.exp(lg * float(BLOCK) * ones_row)
            for b in range(st_ref.shape[0]):
                st = st_ref[b, d, h]
                for step in range(nsub):
                    j = step if d == 0 else nsub - 1 - step
                    rows = slice(j * BLOCK, (j + 1) * BLOCK)
                    s_out[b, j, h] = st.astype(BF16)
                    kz = (k_ref[b, rows, sl].astype(F32) * zeta).astype(BF16)
                    kv = lax.dot_general(kz, v_ref[b, rows, sl], (((0,), (0,)), ((), ())),
                                         preferred_element_type=F32)
                    st = st * chunk_decay + kv
                st_ref[b, d, h] = st


def _ret_states(lg2, rk, rv):
    B, Lp, _ = rk.shape
    tm = _row_tile(Lp)
    nsub = tm // BLOCK
    nt = Lp // tm
    fwd = pl.BlockSpec((B, tm, WR), lambda t: (0, t, 0))
    bwd = pl.BlockSpec((B, tm, WR), lambda t: (0, nt - 1 - t, 0))
    st_shape = (B, nsub, N_HEADS_R, HEAD_DIM_R, HEAD_DIM_R)
    return pl.pallas_call(
        functools.partial(_ret_state_kernel, nsub=nsub),
        grid=(nt,),
        in_specs=[pl.BlockSpec(memory_space=pltpu.SMEM), fwd, fwd, bwd, bwd],
        out_specs=[pl.BlockSpec(st_shape, lambda t: (0, t, 0, 0, 0)),
                   pl.BlockSpec(st_shape, lambda t: (0, nt - 1 - t, 0, 0, 0))],
        out_shape=[jax.ShapeDtypeStruct((B, Lp // BLOCK) + st_shape[2:], BF16)] * 2,
        scratch_shapes=[pltpu.VMEM((B, 2, N_HEADS_R, HEAD_DIM_R, HEAD_DIM_R), F32)],
        compiler_params=_params(("arbitrary",), 48 << 20),
        name="ret_state",
    )(lg2, rk, rv, rk, rv)


def _mixer_kernel(lg_ref, sink_ref, q_ref, kp_ref, kc_ref, kn_ref, vp_ref, vc_ref, vn_ref,
                  rq_ref, rk_ref, rv_ref, gf_ref, gb_ref, sf_ref, sb_ref, bias_ref,
                  h_ref, w_ref, g2_ref, wr_ref, h1_ref, hn_ref, aff_ref,
                  dec_ref, xi_ref, kbuf_ref, vbuf_ref, mix_ref, *, nblocks, tm):
    b = pl.program_id(0)
    c = pl.program_id(1)
    nsub = tm // BLOCK
    kv_w = 3 * BLOCK

    @pl.when((b == 0) & (c == 0))
    def _():
        ii = lax.broadcasted_iota(I32, (BLOCK, BLOCK), 0).astype(F32)
        jj = lax.broadcasted_iota(I32, (BLOCK, BLOCK), 1).astype(F32)
        for d in range(2):
            diff = (ii - jj) if d == 0 else (jj - ii)
            reach = (ii + 1.0) if d == 0 else (float(BLOCK) - ii)
            for h in range(N_HEADS_R):
                lg = lg_ref[d, h]
                dec_ref[d, h] = jnp.where(diff >= 0, jnp.exp(lg * jnp.maximum(diff, 0.0)), 0.0)
                xi_ref[d, h] = jnp.exp(lg * reach)

    for buf, (p_ref, c_ref, n_ref) in ((kbuf_ref, (kp_ref, kc_ref, kn_ref)), (vbuf_ref, (vp_ref, vc_ref, vn_ref))):
        buf[0:BLOCK] = p_ref[0]
        buf[BLOCK:BLOCK + tm] = c_ref[0]
        buf[BLOCK + tm:] = n_ref[0]

    lower_lanes = lax.broadcasted_iota(I32, (1, 2 * HEAD_DIM_A), 1) < HEAD_DIM_A
    upper_rows = lax.broadcasted_iota(I32, (2 * BLOCK, 1), 0) >= BLOCK

    def block(i, carry):
        r0 = pl.multiple_of(i * BLOCK, BLOCK)
        rows = pl.ds(r0, BLOCK)
        win = pl.ds(r0, kv_w)

        n = c * nsub + i
        variant = jnp.where(n < 2, n, jnp.where(n == nblocks - 1, 3, 2))
        for kh in range(N_KV_A):
            g0 = slice(2 * kh * WA_KV, (2 * kh + 1) * WA_KV)
            g1 = slice((2 * kh + 1) * WA_KV, (2 * kh + 2) * WA_KV)
            q2 = jnp.concatenate([q_ref[0, rows, g0], q_ref[0, rows, g1]], axis=0)
            kcat = jnp.concatenate([kbuf_ref[win, g0], kbuf_ref[win, g1]], axis=0)
            vcat = jnp.concatenate([vbuf_ref[win, g0], vbuf_ref[win, g1]], axis=0)
            s = lax.dot_general(q2, kcat, (((1,), (1,)), ((), ())), preferred_element_type=F32)
            s = s + bias_ref[variant, kh]
            probs = []
            denoms = []
            for half in range(2):
                sh = s[:, half * kv_w:(half + 1) * kv_w]
                sk = jnp.where(upper_rows, sink_ref[4 * kh + 2 + half], sink_ref[4 * kh + half])
                m = jnp.maximum(jnp.max(sh, axis=-1, keepdims=True), sk)
                p = jnp.exp2(sh - m)
                denoms.append(jnp.sum(p, axis=-1, keepdims=True) + jnp.exp2(sk - m))
                probs.append(p.astype(BF16))
            o = jnp.dot(jnp.concatenate(probs, axis=1), vcat, preferred_element_type=F32)
            o = o / jnp.where(lower_lanes, denoms[0], denoms[1])
            mix_ref[rows, g0] = o[:BLOCK].astype(BF16)
            mix_ref[rows, g1] = o[BLOCK:].astype(BF16)

        return carry

    def ret_block(i, carry):
        r0 = pl.multiple_of(i * BLOCK, BLOCK)
        rows = pl.ds(r0, BLOCK)
        for h in range(N_HEADS_R):
            sl = slice(h * HEAD_DIM_R, (h + 1) * HEAD_DIM_R)
            qh = rq_ref[0, rows, sl]
            vh = rv_ref[0, rows, sl]
            qk = lax.dot_general(qh, rk_ref[0, rows, sl], (((1,), (1,)), ((), ())),
                                 preferred_element_type=F32)
            qf = qh.astype(F32)
            acc = None
            for d, (st_ref, g_ref) in enumerate(((sf_ref, gf_ref), (sb_ref, gb_ref))):
                lhs = jnp.concatenate([(qk * dec_ref[d, h]).astype(BF16), (qf * xi_ref[d, h]).astype(BF16)], axis=1)
                rhs = jnp.concatenate([vh, st_ref[0, i, h]], axis=0)
                y = jnp.dot(lhs, rhs, preferred_element_type=F32)
                mu = jnp.mean(y, axis=-1, keepdims=True)
                yc = y - mu
                var = jnp.mean(yc * yc, axis=-1, keepdims=True)
                gate = g_ref[0, rows, sl].astype(F32)
                term = gate * jax.nn.sigmoid(gate) * (yc * lax.rsqrt(var + EPS))
                acc = term if acc is None else acc + term
            mix_ref[rows, WA_Q + h * HEAD_DIM_R:WA_Q + (h + 1) * HEAD_DIM_R] = acc.astype(BF16)
        return carry

    for i in range(nsub):
        block(i, 0)
    for i in range(nsub):
        ret_block(i, 0)

    h1 = h_ref[0] + jnp.dot(mix_ref[...], w_ref[...], preferred_element_type=F32)
    h1_ref[0] = h1
    ms = jnp.mean(h1 * h1, axis=-1, keepdims=True)
    xn = h1 * lax.rsqrt(ms + EPS) * g2_ref[...]
    hn_ref[0] = xn.astype(BF16)
    hi = xn.astype(BF16)
    lo = (xn - hi.astype(F32)).astype(BF16)
    both = jnp.dot(hi, wr_ref[...], preferred_element_type=F32)
    logits = (both[:, :LANES] + both[:, LANES:]
              + jnp.dot(lo, wr_ref[:, :LANES], preferred_element_type=F32))
    lt = logits.T[:N_EXPERTS]
    mx = jnp.max(lt, axis=0, keepdims=True)
    ex = jnp.exp(lt - mx)
    aff = ex / jnp.sum(ex, axis=0, keepdims=True)
    col = c * tm + lax.broadcasted_iota(I32, (1, tm), 1)
    aff_ref[0] = jnp.where(col >= PAD_FRONT, aff, -1.0)


def _mixer(lg2, sink, q, k4, v4, rq, rk, rv, gf, gb, sf, sb, bias2, hp, w_out_bf, g2, wr_split):
    B, Lp, D = hp.shape
    nb = Lp // BLOCK
    assert nb >= 3
    tm = _row_tile(Lp)
    nsub = tm // BLOCK
    kvw = k4.shape[-1]
    const = lambda b, c: (0, 0)
    cur = lambda n_: pl.BlockSpec((1, tm, n_), lambda b, c: (b, c, 0))
    prev = pl.BlockSpec((1, BLOCK, kvw), lambda b, c: (b, jnp.maximum(c * nsub - 1, 0), 0))
    nxt = pl.BlockSpec((1, BLOCK, kvw), lambda b, c: (b, jnp.minimum((c + 1) * nsub, nb - 1), 0))
    st = pl.BlockSpec((1, nsub, N_HEADS_R, HEAD_DIM_R, HEAD_DIM_R), lambda b, c: (b, c, 0, 0, 0))
    smem = pl.BlockSpec(memory_space=pltpu.SMEM)
    tab = pltpu.VMEM((2, N_HEADS_R, BLOCK, HEAD_DIM_R), F32)
    kvbuf = pltpu.VMEM((tm + 2 * BLOCK, kvw), BF16)
    return pl.pallas_call(
        functools.partial(_mixer_kernel, nblocks=nb, tm=tm),
        grid=(B, Lp // tm),
        in_specs=[smem, smem, cur(WA_Q), prev, cur(kvw), nxt, prev, cur(kvw), nxt,
                  cur(WR), cur(WR), cur(WR), cur(WR), cur(WR), st, st,
                  pl.BlockSpec(bias2.shape, lambda b, c: (0, 0, 0, 0)),
                  cur(D), pl.BlockSpec((D, D), const), pl.BlockSpec((1, D), const),
                  pl.BlockSpec((D, 2 * LANES), const)],
        out_specs=[cur(D), cur(D), pl.BlockSpec((1, N_EXPERTS, tm), lambda b, c: (b, 0, c))],
        out_shape=[jax.ShapeDtypeStruct((B, Lp, D), F32),
                   jax.ShapeDtypeStruct((B, Lp, D), BF16),
                   jax.ShapeDtypeStruct((B, N_EXPERTS, Lp), F32)],
        scratch_shapes=[tab, tab, kvbuf, kvbuf, pltpu.VMEM((tm, D), BF16)],
        compiler_params=_params(("arbitrary", "arbitrary"), 56 << 20),
        name="mixer",
    )(lg2, sink, q, k4, k4, k4, v4, v4, v4, rq, rk, rv, gf, gb, sf, sb, bias2, hp, w_out_bf, g2, wr_split)


def _route_kernel(aff_ref, posm_ref, cpos_ref, *, cap, nb):
    E = aff_ref.shape[0]

    def keys(sl=slice(None)):
        return pltpu.bitcast(aff_ref[:, sl], I32)

    def count_ge(t):
        return jnp.sum((keys() >= t).astype(I32), axis=-1, keepdims=True)

    def search(_, carry):
        lo, hi = carry
        mid = lo + (hi - lo) // 2
        ok = count_ge(mid) >= cap
        return jnp.where(ok, mid, lo), jnp.where(ok, hi, mid)

    lo0 = jnp.zeros((E, 1), I32)
    hi0 = jnp.full((E, 1), 0x3F800001, I32)
    thr, _ = lax.fori_loop(0, 31, search, (lo0, hi0))
    n_gt = jnp.sum((keys() > thr).astype(F32), axis=-1, keepdims=True)
    need_eq = float(cap) - n_gt

    jr = lax.broadcasted_iota(I32, (LANES, LANES), 0)
    jc = lax.broadcasted_iota(I32, (LANES, LANES), 1)
    upper = (jr < jc).astype(F32).astype(BF16)
    lane = lax.broadcasted_iota(I32, (E, LANES), 1)

    def chunk(c, carry):
        cg, ce, cp = carry
        off = pl.multiple_of(c * LANES, LANES)
        kk = pltpu.bitcast(aff_ref[:, pl.ds(off, LANES)], I32)
        gt = kk > thr
        eq = kk == thr
        both = jnp.concatenate([gt, eq], axis=0).astype(F32)
        ex = jnp.dot(both.astype(BF16), upper, preferred_element_type=F32)
        exg = ex[:E] + cg
        exe = ex[E:] + ce
        sel = gt | (eq & (exe < need_eq))
        pos = exg + jnp.minimum(exe, need_eq)
        posm_ref[:, pl.ds(off, LANES)] = jnp.where(sel, pos, -1.0).astype(I32)
        cp = jnp.where(lane == c, (cg + jnp.minimum(ce, need_eq)).astype(I32), cp)
        tot = jnp.sum(both, axis=-1, keepdims=True)
        return cg + tot[:E], ce + tot[E:], cp

    zero = jnp.zeros((E, 1), F32)
    cg, ce, cp = lax.fori_loop(0, nb, chunk, (zero, zero, jnp.zeros((E, LANES), I32)))
    cpos_ref[...] = jnp.where(lane == nb, (cg + jnp.minimum(ce, need_eq)).astype(I32), cp)


def _route(aff_t, cap):
    B, E, Lp = aff_t.shape
    nb = Lp // LANES
    assert nb + 1 <= LANES
    posm, cpos = pl.pallas_call(
        functools.partial(_route_kernel, cap=cap, nb=nb),
        grid=(1,),
        in_specs=[pl.BlockSpec((B * E, Lp), lambda i: (0, 0))],
        out_specs=[pl.BlockSpec((B * E, Lp), lambda i: (0, 0)),
                   pl.BlockSpec((B * E, LANES), lambda i: (0, 0))],
        out_shape=[jax.ShapeDtypeStruct((B * E, Lp), I32),
                   jax.ShapeDtypeStruct((B * E, LANES), I32)],
        compiler_params=_params(("arbitrary",), 32 << 20),
        name="route",
    )(aff_t.reshape(B * E, Lp))
    return posm.reshape(B, E, Lp), cpos


NARROW_WINDOW = {1: 3 * BF16_SUBLANES, 2: 4 * BF16_SUBLANES}
MAX_CHUNK_BLOCKS = 2
WIDEST_WINDOW = MAX_CHUNK_BLOCKS * BLOCK + BF16_SUBLANES


def _wide_window(nblk):
    return nblk * BLOCK + BF16_SUBLANES


def _chunks(nsub):
    out = []
    i = 0
    while i < nsub:
        n = min(MAX_CHUNK_BLOCKS, nsub - i)
        out.append((i, n))
        i += n
    return out


def _window(cpos_ref, b, e, c, nblk, w, cpad):
    base = (b * N_EXPERTS + e) * LANES + c
    a0 = jnp.minimum((cpos_ref[base] // BF16_SUBLANES) * BF16_SUBLANES, cpad - w)
    return pl.multiple_of(a0, BF16_SUBLANES), cpos_ref[base + nblk] - a0 <= w


def _windows(cpos_ref, b, experts, c, nblk, w, cpad):
    starts = []
    fits = None
    for e in experts:
        a0, ok = _window(cpos_ref, b, e, c, nblk, w, cpad)
        starts.append(a0)
        fits = ok if fits is None else fits & ok
    return starts, fits


EXPERT_GROUP = 8


def _gather_kernel(cpos_ref, hn_ref, posm_ref, x_ref, *, nsub, cpad):
    b = pl.program_id(0)
    g = pl.program_id(1)
    co = pl.program_id(2)
    experts = [g * EXPERT_GROUP + e for e in range(EXPERT_GROUP)]

    @pl.when(co == 0)
    def _():
        x_ref[...] = jnp.zeros_like(x_ref)

    def move(c, nblk, tokens, w, starts=None):
        if starts is None:
            starts = _windows(cpos_ref, b, experts, c, nblk, w, cpad)[0]
        rows = lax.broadcasted_iota(I32, (w, nblk * BLOCK), 0)
        onehots = []
        for e in range(EXPERT_GROUP):
            hit = rows + starts[e] == posm_ref[0, 0, e:e + 1, tokens]
            onehots.append(hit.astype(F32).astype(BF16))
        res = jnp.dot(jnp.concatenate(onehots, axis=0), hn_ref[0, tokens, :],
                      preferred_element_type=F32).astype(BF16)
        for e in range(EXPERT_GROUP):
            x_ref[0, e, pl.ds(starts[e], w), :] += res[e * w:(e + 1) * w]

    for first, nblk in _chunks(nsub):
        c = co * nsub + first
        tokens = slice(first * BLOCK, (first + nblk) * BLOCK)
        narrow_w = NARROW_WINDOW[nblk]
        narrow_starts, narrow = _windows(cpos_ref, b, experts, c, nblk, narrow_w, cpad)
        pl.when(narrow)(functools.partial(move, c, nblk, tokens, narrow_w, narrow_starts))
        pl.when(jnp.logical_not(narrow))(functools.partial(move, c, nblk, tokens, _wide_window(nblk)))


def _gather(cpos_flat, hn, posm, cpad):
    B, Lp, D = hn.shape
    E = N_EXPERTS
    ng = E // EXPERT_GROUP
    tm = _row_tile(Lp)
    posm4 = posm.reshape(B, ng, EXPERT_GROUP, Lp)
    grid_spec = pltpu.PrefetchScalarGridSpec(
        num_scalar_prefetch=1,
        grid=(B, ng, Lp // tm),
        in_specs=[pl.BlockSpec((1, tm, D), lambda b, g, c, cp: (b, c, 0)),
                  pl.BlockSpec((1, 1, EXPERT_GROUP, tm), lambda b, g, c, cp: (b, g, 0, c))],
        out_specs=pl.BlockSpec((1, EXPERT_GROUP, cpad, D), lambda b, g, c, cp: (b, g, 0, 0)),
    )
    return pl.pallas_call(
        functools.partial(_gather_kernel, nsub=tm // BLOCK, cpad=cpad),
        grid_spec=grid_spec,
        out_shape=jax.ShapeDtypeStruct((B, E, cpad, D), BF16),
        compiler_params=_params(("arbitrary", "arbitrary", "arbitrary"), 48 << 20),
        name="gather",
    )(cpos_flat, hn, posm4)


FFN_TILE = 256
FFN_BATCH = 2


def _ffn_kernel(x_ref, wg_ref, wu_ref, wd_ref, y_ref, hm_ref, wd_bf_ref, *, tf):
    f = pl.program_id(2)
    bh, _, cpad, _ = x_ref.shape
    x = x_ref[...].reshape(bh * cpad, D_MODEL)
    a = jnp.dot(x, wg_ref[0, 0].astype(BF16), preferred_element_type=F32)
    u = jnp.dot(x, wu_ref[0, 0].astype(BF16), preferred_element_type=F32)
    off = pl.multiple_of(f * tf, tf)
    hm_ref[:, pl.ds(off, tf)] = (a * jax.nn.sigmoid(a) * u).astype(BF16)
    wd_bf_ref[pl.ds(off, tf), :] = wd_ref[0, 0].astype(BF16)

    @pl.when(f == pl.num_programs(2) - 1)
    def _():
        for i in range(bh):
            y_ref[i, 0] = jnp.dot(hm_ref[i * cpad:(i + 1) * cpad, :], wd_bf_ref[...],
                                  preferred_element_type=F32).astype(BF16)


def _ffn(xe, w_gate, w_up, w_down, layer):
    B, E, cpad, D = xe.shape
    F = w_gate.shape[-1]
    tf = FFN_TILE if F % FFN_TILE == 0 else LANES
    bh = FFN_BATCH if B % FFN_BATCH == 0 else 1
    xspec = pl.BlockSpec((bh, 1, cpad, D), lambda e, b, f: (b, e, 0, 0))
    return pl.pallas_call(
        functools.partial(_ffn_kernel, tf=tf),
        grid=(E, B // bh, F // tf),
        in_specs=[xspec,
                  pl.BlockSpec((1, 1, D, tf), lambda e, b, f: (layer, e, 0, f)),
                  pl.BlockSpec((1, 1, D, tf), lambda e, b, f: (layer, e, 0, f)),
                  pl.BlockSpec((1, 1, tf, D), lambda e, b, f: (layer, e, f, 0))],
        out_specs=xspec,
        out_shape=jax.ShapeDtypeStruct(xe.shape, BF16),
        scratch_shapes=[pltpu.VMEM((bh * cpad, F), BF16), pltpu.VMEM((F, D), BF16)],
        compiler_params=_params(("parallel", "parallel", "arbitrary"), 56 << 20),
        name="ffn",
    )(xe, w_gate, w_up, w_down)


def _combine_kernel(cpos_ref, h_ref, posm_ref, gate_ref, ye_ref, o_ref, *, nsub, chunk0, cpad):
    b = pl.program_id(0)
    co = pl.program_id(1)
    experts = list(range(N_EXPERTS))

    def move(c, nblk, tokens, w, starts=None):
        if starts is None:
            starts = _windows(cpos_ref, b, experts, c, nblk, w, cpad)[0]
        rows = lax.broadcasted_iota(I32, (w, nblk * BLOCK), 0)
        weights = []
        windows = []
        for e in experts:
            hit = rows + starts[e] == posm_ref[0, e:e + 1, tokens]
            weights.append(jnp.where(hit, gate_ref[0, e:e + 1, tokens], 0.0).astype(BF16))
            windows.append(ye_ref[0, e, pl.ds(starts[e], w), :])
        moe = lax.dot_general(jnp.concatenate(weights, axis=0), jnp.concatenate(windows, axis=0),
                              (((0,), (0,)), ((), ())), preferred_element_type=F32)
        o_ref[0, tokens, :] = h_ref[0, tokens, :] + moe

    for first, nblk in _chunks(nsub):
        c = chunk0 + co * nsub + first
        tokens = slice(first * BLOCK, (first + nblk) * BLOCK)
        narrow_w = NARROW_WINDOW[nblk]
        narrow_starts, narrow = _windows(cpos_ref, b, experts, c, nblk, narrow_w, cpad)
        pl.when(narrow)(functools.partial(move, c, nblk, tokens, narrow_w, narrow_starts))
        pl.when(jnp.logical_not(narrow))(functools.partial(move, c, nblk, tokens, _wide_window(nblk)))


def _combine(cpos_flat, h1, posm, gate, ye, cpad, skip_rows=0):
    B, Lp, D = h1.shape
    E = N_EXPERTS
    rows_out = Lp - skip_rows
    tm = _row_tile(rows_out)
    assert skip_rows % BLOCK == 0
    el = lambda *dims: tuple(pl.Element(d) for d in dims)
    first = lambda c: pl.multiple_of(skip_rows + c * tm, BLOCK)
    grid_spec = pltpu.PrefetchScalarGridSpec(
        num_scalar_prefetch=1,
        grid=(B, rows_out // tm),
        in_specs=[pl.BlockSpec(el(1, tm, D), lambda b, c, cp: (b, first(c), 0)),
                  pl.BlockSpec(el(1, E, tm), lambda b, c, cp: (b, 0, first(c))),
                  pl.BlockSpec(el(1, E, tm), lambda b, c, cp: (b, 0, first(c))),
                  pl.BlockSpec((1, E, cpad, D), lambda b, c, cp: (b, 0, 0, 0),
                               pipeline_mode=pl.Buffered(1))],
        out_specs=pl.BlockSpec((1, tm, D), lambda b, c, cp: (b, c, 0)),
    )
    return pl.pallas_call(
        functools.partial(_combine_kernel, nsub=tm // BLOCK, chunk0=skip_rows // BLOCK, cpad=cpad),
        grid_spec=grid_spec,
        out_shape=jax.ShapeDtypeStruct((B, rows_out, D), F32),
        compiler_params=_params(("arbitrary", "arbitrary"), VMEM_LIMIT_CAP),
        name="combine",
    )(cpos_flat, h1, posm, gate, ye)


def _t5_bucket(rel):
    half = N_BUCKETS // 2
    max_exact = half // 2
    n = jnp.abs(rel)
    large = max_exact + (jnp.log(jnp.maximum(n, max_exact).astype(F32) / max_exact)
                         / math.log(MAX_DISTANCE / max_exact) * (half - max_exact)).astype(I32)
    large = jnp.minimum(large, half - 1)
    return jnp.where(rel > 0, half, 0) + jnp.where(n < max_exact, n, large)


def _block_diag_mean(n, group):
    i = jnp.arange(n)
    return jnp.where((i[:, None] // group) == (i[None, :] // group), 1.0 / group, 0.0).astype(BF16)


def kernel(x, meta_tokens, rel_bias, norm1_g, w_in, q_norm_g, k_norm_g, attn_sink, ret_decay,
           w_out, norm2_g, w_router, w_gate, w_up, w_down):
    B, seq, D = x.shape
    depth = w_in.shape[0]
    L = seq + N_META
    Lp = L + PAD_FRONT
    assert D == D_MODEL and Lp % BLOCK == 0
    cap = CAPACITY_FACTOR * L // N_EXPERTS
    cpad = -(-max(cap, WIDEST_WINDOW) // BF16_SUBLANES) * BF16_SUBLANES

    hp = jnp.concatenate([jnp.zeros((B, PAD_FRONT, D), x.dtype),
                          jnp.broadcast_to(meta_tokens.astype(x.dtype)[None], (B, N_META, D)), x], axis=1)

    pos = jnp.arange(Lp, dtype=F32)
    inv = ROPE_BASE ** (-jnp.arange(0, HEAD_DIM_R, 2, dtype=F32) / HEAD_DIM_R)
    ang = pos[:, None] * inv[None]
    cc = jnp.concatenate([jnp.cos(ang), jnp.cos(ang)], axis=-1)
    ss = jnp.concatenate([-jnp.sin(ang), jnp.sin(ang)], axis=-1)
    rel = (jnp.arange(3 * BLOCK)[None, :] - BLOCK) - jnp.arange(BLOCK)[:, None]
    bucket_onehot = (_t5_bucket(rel)[None] == jnp.arange(N_BUCKETS)[:, None, None]).astype(F32)
    bias = jnp.einsum('kqs,kh->hqs', bucket_onehot, rel_bias.astype(F32), precision=lax.Precision.HIGHEST)
    bias = jnp.where((jnp.abs(rel) <= BLOCK)[None], bias * LOG2E, NEG)
    key = jnp.arange(3 * BLOCK)
    bias = jnp.stack([jnp.where(key >= BLOCK + PAD_FRONT, bias, NEG), jnp.where(key >= PAD_FRONT, bias, NEG),
                      bias, jnp.where(key < 2 * BLOCK, bias, NEG)])
    bias = bias.reshape(4, N_KV_A, 2, 2, BLOCK, 3 * BLOCK).transpose(0, 1, 2, 4, 3, 5)
    bias = bias.reshape(4, N_KV_A, 2 * BLOCK, 6 * BLOCK)
    bdq = _block_diag_mean(WA_Q, HEAD_DIM_A)
    bdk = _block_diag_mean(WA_KV, HEAD_DIM_A)

    for l in range(depth):
        qg = (jnp.tile(q_norm_g[l].astype(F32), N_HEADS_A) * (HEAD_DIM_A ** -0.5 * LOG2E))[None]
        kg = jnp.tile(k_norm_g[l].astype(F32), N_KV_A)[None]
        lg2 = -jnp.exp(ret_decay[l].astype(F32))
        q, k, v, rq, rk, rv, gf, gb = _in_proj(hp, norm1_g[l][None].astype(F32), w_in[l].astype(BF16),
                                               cc, ss, qg, kg, bdq, bdk)
        sf, sb = _ret_states(lg2, rk, rv)
        wr = jnp.pad(w_router[l].astype(F32), ((0, 0), (0, LANES - N_EXPERTS)))
        wr_hi = wr.astype(BF16)
        wr_lo = (wr - wr_hi.astype(F32)).astype(BF16)
        wr_split = jnp.concatenate([wr_hi, wr_lo], axis=1)
        h1, hn, aff = _mixer(lg2, attn_sink[l].astype(F32) * LOG2E, q, k, v, rq, rk, rv, gf, gb, sf, sb, bias,
                             hp, w_out[l].astype(BF16), norm2_g[l][None].astype(F32), wr_split)
        posm, cpos = _route(aff, cap)
        cpos_flat = cpos.reshape(-1)
        xe = _gather(cpos_flat, hn, posm, cpad)
        ye = _ffn(xe, w_gate, w_up, w_down, l)
        last = l == depth - 1
        hp = _combine(cpos_flat, h1, posm, aff, ye, cpad, skip_rows=(PAD_FRONT + N_META) if last else 0)
    return hp
```

```python
import functools
import math

import jax
import jax.numpy as jnp
from jax import lax
from jax.experimental import pallas as pl
from jax.experimental.pallas import tpu as pltpu

D_MODEL = 1024
N_META = 16
BLOCK = 128
PAD_FRONT = BLOCK - N_META
EPS = 1e-6
N_HEADS_A = 8
N_KV_A = 2
GROUP_A = N_HEADS_A // N_KV_A
HEAD_DIM_A = 64
N_HEADS_R = 4
HEAD_DIM_R = 128
ROPE_BASE = 10000.0
N_BUCKETS = 32
MAX_DISTANCE = 128
N_EXPERTS = 16
CAPACITY_FACTOR = 2
WA_Q = N_HEADS_A * HEAD_DIM_A
WA_KV = N_KV_A * HEAD_DIM_A
WR = N_HEADS_R * HEAD_DIM_R
D_IN = WA_Q + 2 * WA_KV + 5 * WR
NEG = -1e30
LOG2E = math.log2(math.e)

LANES = 128
BF16_SUBLANES = 16
VMEM_LIMIT_CAP = 60000 * 1024

F32 = jnp.float32
BF16 = jnp.bfloat16
I32 = jnp.int32


def _params(semantics, vmem_bytes, **extra):
    return pltpu.CompilerParams(dimension_semantics=semantics,
                                vmem_limit_bytes=min(int(vmem_bytes), VMEM_LIMIT_CAP), **extra)


def _row_tile(lp):
    best = BLOCK
    for t in range(BLOCK, 1024 + 1, BLOCK):
        if lp % t == 0:
            best = t
    return best


def _split_dot(x, w_ref):
    hi = x.astype(BF16)
    lo = (x - hi.astype(F32)).astype(BF16)
    w = w_ref[...]
    return (jnp.dot(hi, w, preferred_element_type=F32) + jnp.dot(lo, w, preferred_element_type=F32))


HEAD_ROWS = PAD_FRONT + N_META


def _residual_tile(h_ref, head_ref, c, tm, from_x):
    blk = h_ref[0]
    if not from_x:
        return blk
    first = jnp.concatenate([head_ref[...], blk[:tm - HEAD_ROWS]], axis=0)
    return jnp.where(c == 0, first, blk)


def _residual_spec(tm, d, from_x):
    if not from_x:
        return pl.BlockSpec((1, tm, d), lambda b, c: (b, c, 0))
    start = lambda c: pl.multiple_of(jnp.maximum(c * tm - HEAD_ROWS, 0), BLOCK)
    return pl.BlockSpec((pl.Element(1), pl.Element(tm), pl.Element(d)), lambda b, c: (b, start(c), 0))


def _in_proj_kernel(h_ref, head_ref, g1_ref, w_ref, cc_ref, ss_ref, qg_ref, kg_ref, bdq_ref, bdk_ref,
                    q_ref, k_ref, v_ref, rq_ref, rk_ref, rv_ref, gf_ref, gb_ref, *, tm, from_x):
    j = pl.program_id(1)
    x = _residual_tile(h_ref, head_ref, j, tm, from_x)
    ms = jnp.mean(x * x, axis=-1, keepdims=True)
    xn = x * lax.rsqrt(ms + EPS) * g1_ref[...]
    row = j * tm + lax.broadcasted_iota(I32, (tm, 1), 0)
    xn = jnp.where(row >= PAD_FRONT, xn, 0.0).astype(BF16)

    def proj(lo, n):
        return jnp.dot(xn, w_ref[:, lo:lo + n], preferred_element_type=F32)

    q = proj(0, WA_Q)
    q_ms = jnp.dot((q * q).astype(BF16), bdq_ref[...], preferred_element_type=F32)
    q_ref[0] = (q * lax.rsqrt(q_ms + EPS) * qg_ref[...]).astype(BF16)
    k = proj(WA_Q, WA_KV)
    k = k * lax.rsqrt(_split_dot(k * k, bdk_ref) + EPS) * kg_ref[...]
    v = proj(WA_Q + WA_KV, WA_KV)
    lower = lax.broadcasted_iota(I32, (1, WA_KV), 1) < HEAD_DIM_A
    for t, ref in ((k, k_ref), (v, v_ref)):
        head0_lo = jnp.where(lower, t, 0.0)
        head1_hi = jnp.where(lower, 0.0, t)
        variants = (head0_lo, pltpu.roll(head0_lo, HEAD_DIM_A, 1), pltpu.roll(head1_hi, HEAD_DIM_A, 1), head1_hi)
        for i, var in enumerate(variants):
            ref[0, :, i * WA_KV:(i + 1) * WA_KV] = var.astype(BF16)

    cc = cc_ref[...]
    ss = ss_ref[...]
    base = WA_Q + 2 * WA_KV
    for off, ref, scale in ((base, rq_ref, 1.0), (base + WR, rk_ref, HEAD_DIM_R ** -0.5)):
        r = proj(off, WR)
        for hh in range(N_HEADS_R):
            seg = r[:, hh * HEAD_DIM_R:(hh + 1) * HEAD_DIM_R]
            rot = seg * cc + pltpu.roll(seg, HEAD_DIM_R // 2, 1) * ss
            ref[0, :, hh * HEAD_DIM_R:(hh + 1) * HEAD_DIM_R] = (rot * scale).astype(BF16)
    rv_ref[0] = proj(base + 2 * WR, WR).astype(BF16)
    gf_ref[0] = proj(base + 3 * WR, WR).astype(BF16)
    gb_ref[0] = proj(base + 4 * WR, WR).astype(BF16)


def _in_proj(h, head, from_x, g1, w_in_bf, cc, ss, qg, kg, bdq, bdk):
    B, D = h.shape[0], h.shape[2]
    Lp = cc.shape[0]
    tm = _row_tile(Lp)
    const = lambda b, j: (0, 0)
    row = lambda n: pl.BlockSpec((1, tm, n), lambda b, j: (b, j, 0))
    widths = (WA_Q, 2 * N_KV_A * WA_KV, 2 * N_KV_A * WA_KV, WR, WR, WR, WR, WR)
    return pl.pallas_call(
        functools.partial(_in_proj_kernel, tm=tm, from_x=from_x),
        grid=(B, Lp // tm),
        in_specs=[_residual_spec(tm, D, from_x),
                  pl.BlockSpec((HEAD_ROWS, D), const),
                  pl.BlockSpec((1, D), const),
                  pl.BlockSpec((D, D_IN), const),
                  pl.BlockSpec((tm, HEAD_DIM_R), lambda b, j: (j, 0)),
                  pl.BlockSpec((tm, HEAD_DIM_R), lambda b, j: (j, 0)),
                  pl.BlockSpec((1, WA_Q), const),
                  pl.BlockSpec((1, WA_KV), const),
                  pl.BlockSpec((WA_Q, WA_Q), const),
                  pl.BlockSpec((WA_KV, WA_KV), const)],
        out_specs=[row(n) for n in widths],
        out_shape=[jax.ShapeDtypeStruct((B, Lp, n), BF16) for n in widths],
        compiler_params=_params(("parallel", "parallel"), 48 << 20),
        name="in_proj",
    )(h, head, g1, w_in_bf, cc, ss, qg, kg, bdq, bdk)


def _ret_state_kernel(lg_ref, kf_ref, vf_ref, kb_ref, vb_ref, sf_ref, sb_ref, st_ref, *, nsub):
    t = pl.program_id(0)

    @pl.when(t == 0)
    def _():
        st_ref[...] = jnp.zeros_like(st_ref)

    idx = lax.broadcasted_iota(I32, (BLOCK, 1), 0).astype(F32)
    ones_row = jnp.ones((1, HEAD_DIM_R), F32)
    for d, (k_ref, v_ref, s_out) in enumerate(((kf_ref, vf_ref, sf_ref), (kb_ref, vb_ref, sb_ref))):
        for h in range(N_HEADS_R):
            lg = lg_ref[d, h]
            sl = slice(h * HEAD_DIM_R, (h + 1) * HEAD_DIM_R)
            zeta = jnp.exp(lg * ((BLOCK - 1.0 - idx) if d == 0 else idx))
            chunk_decay = jnp.exp(lg * float(BLOCK) * ones_row)
            for b in range(st_ref.shape[0]):
                st = st_ref[b, d, h]
                for step in range(nsub):
                    j = step if d == 0 else nsub - 1 - step
                    rows = slice(j * BLOCK, (j + 1) * BLOCK)
                    s_out[b, j, h] = st.astype(BF16)
                    kz = (k_ref[b, rows, sl].astype(F32) * zeta).astype(BF16)
                    kv = lax.dot_general(kz, v_ref[b, rows, sl], (((0,), (0,)), ((), ())),
                                         preferred_element_type=F32)
                    st = st * chunk_decay + kv
                st_ref[b, d, h] = st


def _ret_states(lg2, rk, rv):
    B, Lp, _ = rk.shape
    tm = _row_tile(Lp)
    nsub = tm // BLOCK
    nt = Lp // tm
    fwd = pl.BlockSpec((B, tm, WR), lambda t: (0, t, 0))
    bwd = pl.BlockSpec((B, tm, WR), lambda t: (0, nt - 1 - t, 0))
    st_shape = (B, nsub, N_HEADS_R, HEAD_DIM_R, HEAD_DIM_R)
    return pl.pallas_call(
        functools.partial(_ret_state_kernel, nsub=nsub),
        grid=(nt,),
        in_specs=[pl.BlockSpec(memory_space=pltpu.SMEM), fwd, fwd, bwd, bwd],
        out_specs=[pl.BlockSpec(st_shape, lambda t: (0, t, 0, 0, 0)),
                   pl.BlockSpec(st_shape, lambda t: (0, nt - 1 - t, 0, 0, 0))],
        out_shape=[jax.ShapeDtypeStruct((B, Lp // BLOCK) + st_shape[2:], BF16)] * 2,
        scratch_shapes=[pltpu.VMEM((B, 2, N_HEADS_R, HEAD_DIM_R, HEAD_DIM_R), F32)],
        compiler_params=_params(("arbitrary",), 48 << 20),
        name="ret_state",
    )(lg2, rk, rv, rk, rv)


def _mixer_kernel(lg_ref, sink_ref, q_ref, kp_ref, kc_ref, kn_ref, vp_ref, vc_ref, vn_ref,
                  rq_ref, rk_ref, rv_ref, gf_ref, gb_ref, sf_ref, sb_ref, bias_ref,
                  h_ref, head_ref, w_ref, g2_ref, wr_ref, h1_ref, hn_ref, aff_ref,
                  dec_ref, xi_ref, kbuf_ref, vbuf_ref, mix_ref, *, nblocks, tm, from_x):
    b = pl.program_id(0)
    c = pl.program_id(1)
    nsub = tm // BLOCK
    kv_w = 3 * BLOCK

    @pl.when((b == 0) & (c == 0))
    def _():
        ii = lax.broadcasted_iota(I32, (BLOCK, BLOCK), 0).astype(F32)
        jj = lax.broadcasted_iota(I32, (BLOCK, BLOCK), 1).astype(F32)
        for d in range(2):
            diff = (ii - jj) if d == 0 else (jj - ii)
            reach = (ii + 1.0) if d == 0 else (float(BLOCK) - ii)
            for h in range(N_HEADS_R):
                lg = lg_ref[d, h]
                dec_ref[d, h] = jnp.where(diff >= 0, jnp.exp(lg * jnp.maximum(diff, 0.0)), 0.0)
                xi_ref[d, h] = jnp.exp(lg * reach)

    for buf, (p_ref, c_ref, n_ref) in ((kbuf_ref, (kp_ref, kc_ref, kn_ref)), (vbuf_ref, (vp_ref, vc_ref, vn_ref))):
        buf[0:BLOCK] = p_ref[0]
        buf[BLOCK:BLOCK + tm] = c_ref[0]
        buf[BLOCK + tm:] = n_ref[0]

    lower_lanes = lax.broadcasted_iota(I32, (1, 2 * HEAD_DIM_A), 1) < HEAD_DIM_A
    upper_rows = lax.broadcasted_iota(I32, (2 * BLOCK, 1), 0) >= BLOCK

    def block(i, carry):
        r0 = pl.multiple_of(i * BLOCK, BLOCK)
        rows = pl.ds(r0, BLOCK)
        win = pl.ds(r0, kv_w)

        n = c * nsub + i
        variant = jnp.where(n < 2, n, jnp.where(n == nblocks - 1, 3, 2))
        for kh in range(N_KV_A):
            g0 = slice(2 * kh * WA_KV, (2 * kh + 1) * WA_KV)
            g1 = slice((2 * kh + 1) * WA_KV, (2 * kh + 2) * WA_KV)
            q2 = jnp.concatenate([q_ref[0, rows, g0], q_ref[0, rows, g1]], axis=0)
            kcat = jnp.concatenate([kbuf_ref[win, g0], kbuf_ref[win, g1]], axis=0)
            vcat = jnp.concatenate([vbuf_ref[win, g0], vbuf_ref[win, g1]], axis=0)
            s = lax.dot_general(q2, kcat, (((1,), (1,)), ((), ())), preferred_element_type=F32)
            s = s + bias_ref[variant, kh]
            probs = []
            denoms = []
            for half in range(2):
                sh = s[:, half * kv_w:(half + 1) * kv_w]
                sk = jnp.where(upper_rows, sink_ref[4 * kh + 2 + half], sink_ref[4 * kh + half])
                m = jnp.maximum(jnp.max(sh, axis=-1, keepdims=True), sk)
                p = jnp.exp2(sh - m)
                denoms.append(jnp.sum(p, axis=-1, keepdims=True) + jnp.exp2(sk - m))
                probs.append(p.astype(BF16))
            o = jnp.dot(jnp.concatenate(probs, axis=1), vcat, preferred_element_type=F32)
            o = o / jnp.where(lower_lanes, denoms[0], denoms[1])
            mix_ref[rows, g0] = o[:BLOCK].astype(BF16)
            mix_ref[rows, g1] = o[BLOCK:].astype(BF16)

        return carry

    def ret_block(i, carry):
        r0 = pl.multiple_of(i * BLOCK, BLOCK)
        rows = pl.ds(r0, BLOCK)
        for h in range(N_HEADS_R):
            sl = slice(h * HEAD_DIM_R, (h + 1) * HEAD_DIM_R)
            qh = rq_ref[0, rows, sl]
            vh = rv_ref[0, rows, sl]
            qk = lax.dot_general(qh, rk_ref[0, rows, sl], (((1,), (1,)), ((), ())),
                                 preferred_element_type=F32)
            qf = qh.astype(F32)
            acc = None
            for d, (st_ref, g_ref) in enumerate(((sf_ref, gf_ref), (sb_ref, gb_ref))):
                lhs = jnp.concatenate([(qk * dec_ref[d, h]).astype(BF16), (qf * xi_ref[d, h]).astype(BF16)], axis=1)
                rhs = jnp.concatenate([vh, st_ref[0, i, h]], axis=0)
                y = jnp.dot(lhs, rhs, preferred_element_type=F32)
                mu = jnp.mean(y, axis=-1, keepdims=True)
                yc = y - mu
                var = jnp.mean(yc * yc, axis=-1, keepdims=True)
                gate = g_ref[0, rows, sl].astype(F32)
                term = gate * jax.nn.sigmoid(gate) * (yc * lax.rsqrt(var + EPS))
                acc = term if acc is None else acc + term
            mix_ref[rows, WA_Q + h * HEAD_DIM_R:WA_Q + (h + 1) * HEAD_DIM_R] = acc.astype(BF16)
        return carry

    for i in range(nsub):
        block(i, 0)
    for i in range(nsub):
        ret_block(i, 0)

    h1 = (_residual_tile(h_ref, head_ref, c, tm, from_x)
          + jnp.dot(mix_ref[...], w_ref[...], preferred_element_type=F32))
    h1_ref[0] = h1
    ms = jnp.mean(h1 * h1, axis=-1, keepdims=True)
    xn = h1 * lax.rsqrt(ms + EPS) * g2_ref[...]
    hn_ref[0] = xn.astype(BF16)
    hi = xn.astype(BF16)
    lo = (xn - hi.astype(F32)).astype(BF16)
    both = jnp.dot(hi, wr_ref[...], preferred_element_type=F32)
    logits = (both[:, :LANES] + both[:, LANES:]
              + jnp.dot(lo, wr_ref[:, :LANES], preferred_element_type=F32))
    lt = logits.T[:N_EXPERTS]
    mx = jnp.max(lt, axis=0, keepdims=True)
    ex = jnp.exp(lt - mx)
    aff = ex / jnp.sum(ex, axis=0, keepdims=True)
    col = c * tm + lax.broadcasted_iota(I32, (1, tm), 1)
    aff_ref[0] = jnp.where(col >= PAD_FRONT, aff, -1.0)


def _mixer(lg2, sink, q, k4, v4, rq, rk, rv, gf, gb, sf, sb, bias2, h, head, from_x, w_out_bf, g2, wr_split):
    B, Lp, _ = q.shape
    D = h.shape[2]
    nb = Lp // BLOCK
    assert nb >= 3
    tm = _row_tile(Lp)
    nsub = tm // BLOCK
    kvw = k4.shape[-1]
    const = lambda b, c: (0, 0)
    cur = lambda n_: pl.BlockSpec((1, tm, n_), lambda b, c: (b, c, 0))
    prev = pl.BlockSpec((1, BLOCK, kvw), lambda b, c: (b, jnp.maximum(c * nsub - 1, 0), 0))
    nxt = pl.BlockSpec((1, BLOCK, kvw), lambda b, c: (b, jnp.minimum((c + 1) * nsub, nb - 1), 0))
    st = pl.BlockSpec((1, nsub, N_HEADS_R, HEAD_DIM_R, HEAD_DIM_R), lambda b, c: (b, c, 0, 0, 0))
    smem = pl.BlockSpec(memory_space=pltpu.SMEM)
    tab = pltpu.VMEM((2, N_HEADS_R, BLOCK, HEAD_DIM_R), F32)
    kvbuf = pltpu.VMEM((tm + 2 * BLOCK, kvw), BF16)
    return pl.pallas_call(
        functools.partial(_mixer_kernel, nblocks=nb, tm=tm, from_x=from_x),
        grid=(B, Lp // tm),
        in_specs=[smem, smem, cur(WA_Q), prev, cur(kvw), nxt, prev, cur(kvw), nxt,
                  cur(WR), cur(WR), cur(WR), cur(WR), cur(WR), st, st,
                  pl.BlockSpec(bias2.shape, lambda b, c: (0, 0, 0, 0)),
                  _residual_spec(tm, D, from_x), pl.BlockSpec((HEAD_ROWS, D), const),
                  pl.BlockSpec((D, D), const), pl.BlockSpec((1, D), const),
                  pl.BlockSpec((D, 2 * LANES), const)],
        out_specs=[cur(D), cur(D), pl.BlockSpec((1, N_EXPERTS, tm), lambda b, c: (b, 0, c))],
        out_shape=[jax.ShapeDtypeStruct((B, Lp, D), F32),
                   jax.ShapeDtypeStruct((B, Lp, D), BF16),
                   jax.ShapeDtypeStruct((B, N_EXPERTS, Lp), F32)],
        scratch_shapes=[tab, tab, kvbuf, kvbuf, pltpu.VMEM((tm, D), BF16)],
        compiler_params=_params(("arbitrary", "arbitrary"), 56 << 20),
        name="mixer",
    )(lg2, sink, q, k4, k4, k4, v4, v4, v4, rq, rk, rv, gf, gb, sf, sb, bias2, h, head, w_out_bf, g2, wr_split)


def _route_kernel(aff_ref, posm_ref, cpos_ref, *, cap, nb):
    E = aff_ref.shape[0]

    def keys(sl=slice(None)):
        return pltpu.bitcast(aff_ref[:, sl], I32)

    def count_ge(t):
        return jnp.sum((keys() >= t).astype(I32), axis=-1, keepdims=True)

    def search(_, carry):
        lo, hi = carry
        mid = lo + (hi - lo) // 2
        ok = count_ge(mid) >= cap
        return jnp.where(ok, mid, lo), jnp.where(ok, hi, mid)

    lo0 = jnp.zeros((E, 1), I32)
    hi0 = jnp.full((E, 1), 0x3F800001, I32)
    thr, _ = lax.fori_loop(0, 31, search, (lo0, hi0))
    n_gt = jnp.sum((keys() > thr).astype(F32), axis=-1, keepdims=True)
    need_eq = float(cap) - n_gt

    jr = lax.broadcasted_iota(I32, (LANES, LANES), 0)
    jc = lax.broadcasted_iota(I32, (LANES, LANES), 1)
    upper = (jr < jc).astype(F32).astype(BF16)
    lane = lax.broadcasted_iota(I32, (E, LANES), 1)

    def chunk(c, carry):
        cg, ce, cp = carry
        off = pl.multiple_of(c * LANES, LANES)
        kk = pltpu.bitcast(aff_ref[:, pl.ds(off, LANES)], I32)
        gt = kk > thr
        eq = kk == thr
        both = jnp.concatenate([gt, eq], axis=0).astype(F32)
        ex = jnp.dot(both.astype(BF16), upper, preferred_element_type=F32)
        exg = ex[:E] + cg
        exe = ex[E:] + ce
        sel = gt | (eq & (exe < need_eq))
        pos = exg + jnp.minimum(exe, need_eq)
        posm_ref[:, pl.ds(off, LANES)] = jnp.where(sel, pos, -1.0).astype(I32)
        cp = jnp.where(lane == c, (cg + jnp.minimum(ce, need_eq)).astype(I32), cp)
        tot = jnp.sum(both, axis=-1, keepdims=True)
        return cg + tot[:E], ce + tot[E:], cp

    zero = jnp.zeros((E, 1), F32)
    cg, ce, cp = lax.fori_loop(0, nb, chunk, (zero, zero, jnp.zeros((E, LANES), I32)))
    cpos_ref[...] = jnp.where(lane == nb, (cg + jnp.minimum(ce, need_eq)).astype(I32), cp)


def _route(aff_t, cap):
    B, E, Lp = aff_t.shape
    nb = Lp // LANES
    assert nb + 1 <= LANES
    posm, cpos = pl.pallas_call(
        functools.partial(_route_kernel, cap=cap, nb=nb),
        grid=(1,),
        in_specs=[pl.BlockSpec((B * E, Lp), lambda i: (0, 0))],
        out_specs=[pl.BlockSpec((B * E, Lp), lambda i: (0, 0)),
                   pl.BlockSpec((B * E, LANES), lambda i: (0, 0))],
        out_shape=[jax.ShapeDtypeStruct((B * E, Lp), I32),
                   jax.ShapeDtypeStruct((B * E, LANES), I32)],
        compiler_params=_params(("arbitrary",), 32 << 20),
        name="route",
    )(aff_t.reshape(B * E, Lp))
    return posm.reshape(B, E, Lp), cpos


NARROW_WINDOW = {1: 3 * BF16_SUBLANES, 2: 4 * BF16_SUBLANES}
MAX_CHUNK_BLOCKS = 2
WIDEST_WINDOW = MAX_CHUNK_BLOCKS * BLOCK + BF16_SUBLANES


def _wide_window(nblk):
    return nblk * BLOCK + BF16_SUBLANES


def _chunks(nsub):
    out = []
    i = 0
    while i < nsub:
        n = min(MAX_CHUNK_BLOCKS, nsub - i)
        out.append((i, n))
        i += n
    return out


def _window(cpos_ref, b, e, c, nblk, w, cpad):
    base = (b * N_EXPERTS + e) * LANES + c
    a0 = jnp.minimum((cpos_ref[base] // BF16_SUBLANES) * BF16_SUBLANES, cpad - w)
    return pl.multiple_of(a0, BF16_SUBLANES), cpos_ref[base + nblk] - a0 <= w


def _windows(cpos_ref, b, experts, c, nblk, w, cpad):
    starts = []
    fits = None
    for e in experts:
        a0, ok = _window(cpos_ref, b, e, c, nblk, w, cpad)
        starts.append(a0)
        fits = ok if fits is None else fits & ok
    return starts, fits


EXPERT_GROUP = 8


def _gather_kernel(cpos_ref, hn_ref, posm_ref, x_ref, *, nsub, cpad):
    b = pl.program_id(0)
    g = pl.program_id(1)
    co = pl.program_id(2)
    experts = [g * EXPERT_GROUP + e for e in range(EXPERT_GROUP)]

    @pl.when(co == 0)
    def _():
        x_ref[...] = jnp.zeros_like(x_ref)

    def move(c, nblk, tokens, w, starts=None):
        if starts is None:
            starts = _windows(cpos_ref, b, experts, c, nblk, w, cpad)[0]
        rows = lax.broadcasted_iota(I32, (w, nblk * BLOCK), 0)
        onehots = []
        for e in range(EXPERT_GROUP):
            hit = rows + starts[e] == posm_ref[0, 0, e:e + 1, tokens]
            onehots.append(hit.astype(F32).astype(BF16))
        res = jnp.dot(jnp.concatenate(onehots, axis=0), hn_ref[0, tokens, :],
                      preferred_element_type=F32).astype(BF16)
        for e in range(EXPERT_GROUP):
            x_ref[0, e, pl.ds(starts[e], w), :] += res[e * w:(e + 1) * w]

    for first, nblk in _chunks(nsub):
        c = co * nsub + first
        tokens = slice(first * BLOCK, (first + nblk) * BLOCK)
        narrow_w = NARROW_WINDOW[nblk]
        narrow_starts, narrow = _windows(cpos_ref, b, experts, c, nblk, narrow_w, cpad)
        pl.when(narrow)(functools.partial(move, c, nblk, tokens, narrow_w, narrow_starts))
        pl.when(jnp.logical_not(narrow))(functools.partial(move, c, nblk, tokens, _wide_window(nblk)))


def _gather(cpos_flat, hn, posm, cpad):
    B, Lp, D = hn.shape
    E = N_EXPERTS
    ng = E // EXPERT_GROUP
    tm = _row_tile(Lp)
    posm4 = posm.reshape(B, ng, EXPERT_GROUP, Lp)
    grid_spec = pltpu.PrefetchScalarGridSpec(
        num_scalar_prefetch=1,
        grid=(B, ng, Lp // tm),
        in_specs=[pl.BlockSpec((1, tm, D), lambda b, g, c, cp: (b, c, 0)),
                  pl.BlockSpec((1, 1, EXPERT_GROUP, tm), lambda b, g, c, cp: (b, g, 0, c))],
        out_specs=pl.BlockSpec((1, EXPERT_GROUP, cpad, D), lambda b, g, c, cp: (b, g, 0, 0)),
    )
    return pl.pallas_call(
        functools.partial(_gather_kernel, nsub=tm // BLOCK, cpad=cpad),
        grid_spec=grid_spec,
        out_shape=jax.ShapeDtypeStruct((B, E, cpad, D), BF16),
        compiler_params=_params(("arbitrary", "arbitrary", "arbitrary"), 48 << 20),
        name="gather",
    )(cpos_flat, hn, posm4)


FFN_TILE = 256
FFN_BATCH = 2


def _ffn_kernel(x_ref, wg_ref, wu_ref, wd_ref, y_ref, hm_ref, wd_bf_ref, *, tf):
    f = pl.program_id(2)
    bh, _, cpad, _ = x_ref.shape
    x = x_ref[...].reshape(bh * cpad, D_MODEL)
    a = jnp.dot(x, wg_ref[0, 0].astype(BF16), preferred_element_type=F32)
    u = jnp.dot(x, wu_ref[0, 0].astype(BF16), preferred_element_type=F32)
    off = pl.multiple_of(f * tf, tf)
    hm_ref[:, pl.ds(off, tf)] = (a * jax.nn.sigmoid(a) * u).astype(BF16)
    wd_bf_ref[pl.ds(off, tf), :] = wd_ref[0, 0].astype(BF16)

    @pl.when(f == pl.num_programs(2) - 1)
    def _():
        for i in range(bh):
            y_ref[i, 0] = jnp.dot(hm_ref[i * cpad:(i + 1) * cpad, :], wd_bf_ref[...],
                                  preferred_element_type=F32).astype(BF16)


def _ffn(xe, w_gate, w_up, w_down, layer):
    B, E, cpad, D = xe.shape
    F = w_gate.shape[-1]
    tf = FFN_TILE if F % FFN_TILE == 0 else LANES
    bh = FFN_BATCH if B % FFN_BATCH == 0 else 1
    xspec = pl.BlockSpec((bh, 1, cpad, D), lambda e, b, f: (b, e, 0, 0))
    return pl.pallas_call(
        functools.partial(_ffn_kernel, tf=tf),
        grid=(E, B // bh, F // tf),
        in_specs=[xspec,
                  pl.BlockSpec((1, 1, D, tf), lambda e, b, f: (layer, e, 0, f)),
                  pl.BlockSpec((1, 1, D, tf), lambda e, b, f: (layer, e, 0, f)),
                  pl.BlockSpec((1, 1, tf, D), lambda e, b, f: (layer, e, f, 0))],
        out_specs=xspec,
        out_shape=jax.ShapeDtypeStruct(xe.shape, BF16),
        scratch_shapes=[pltpu.VMEM((bh * cpad, F), BF16), pltpu.VMEM((F, D), BF16)],
        compiler_params=_params(("parallel", "parallel", "arbitrary"), 56 << 20),
        name="ffn",
    )(xe, w_gate, w_up, w_down)


def _combine_kernel(cpos_ref, h_ref, posm_ref, gate_ref, ye_ref, o_ref, *, nsub, chunk0, cpad):
    b = pl.program_id(0)
    co = pl.program_id(1)
    experts = list(range(N_EXPERTS))

    def move(c, nblk, tokens, w, starts=None):
        if starts is None:
            starts = _windows(cpos_ref, b, experts, c, nblk, w, cpad)[0]
        rows = lax.broadcasted_iota(I32, (w, nblk * BLOCK), 0)
        weights = []
        windows = []
        for e in experts:
            hit = rows + starts[e] == posm_ref[0, e:e + 1, tokens]
            weights.append(jnp.where(hit, gate_ref[0, e:e + 1, tokens], 0.0).astype(BF16))
            windows.append(ye_ref[0, e, pl.ds(starts[e], w), :])
        moe = lax.dot_general(jnp.concatenate(weights, axis=0), jnp.concatenate(windows, axis=0),
                              (((0,), (0,)), ((), ())), preferred_element_type=F32)
        o_ref[0, tokens, :] = h_ref[0, tokens, :] + moe

    for first, nblk in _chunks(nsub):
        c = chunk0 + co * nsub + first
        tokens = slice(first * BLOCK, (first + nblk) * BLOCK)
        narrow_w = NARROW_WINDOW[nblk]
        narrow_starts, narrow = _windows(cpos_ref, b, experts, c, nblk, narrow_w, cpad)
        pl.when(narrow)(functools.partial(move, c, nblk, tokens, narrow_w, narrow_starts))
        pl.when(jnp.logical_not(narrow))(functools.partial(move, c, nblk, tokens, _wide_window(nblk)))


def _combine(cpos_flat, h1, posm, gate, ye, cpad, skip_rows=0):
    B, Lp, D = h1.shape
    E = N_EXPERTS
    rows_out = Lp - skip_rows
    tm = _row_tile(rows_out)
    assert skip_rows % BLOCK == 0
    el = lambda *dims: tuple(pl.Element(d) for d in dims)
    first = lambda c: pl.multiple_of(skip_rows + c * tm, BLOCK)
    grid_spec = pltpu.PrefetchScalarGridSpec(
        num_scalar_prefetch=1,
        grid=(B, rows_out // tm),
        in_specs=[pl.BlockSpec(el(1, tm, D), lambda b, c, cp: (b, first(c), 0)),
                  pl.BlockSpec(el(1, E, tm), lambda b, c, cp: (b, 0, first(c))),
                  pl.BlockSpec(el(1, E, tm), lambda b, c, cp: (b, 0, first(c))),
                  pl.BlockSpec((1, E, cpad, D), lambda b, c, cp: (b, 0, 0, 0),
                               pipeline_mode=pl.Buffered(1))],
        out_specs=pl.BlockSpec((1, tm, D), lambda b, c, cp: (b, c, 0)),
    )
    return pl.pallas_call(
        functools.partial(_combine_kernel, nsub=tm // BLOCK, chunk0=skip_rows // BLOCK, cpad=cpad),
        grid_spec=grid_spec,
        out_shape=jax.ShapeDtypeStruct((B, rows_out, D), F32),
        compiler_params=_params(("arbitrary", "arbitrary"), VMEM_LIMIT_CAP),
        name="combine",
    )(cpos_flat, h1, posm, gate, ye)


def _t5_bucket(rel):
    half = N_BUCKETS // 2
    max_exact = half // 2
    n = jnp.abs(rel)
    large = max_exact + (jnp.log(jnp.maximum(n, max_exact).astype(F32) / max_exact)
                         / math.log(MAX_DISTANCE / max_exact) * (half - max_exact)).astype(I32)
    large = jnp.minimum(large, half - 1)
    return jnp.where(rel > 0, half, 0) + jnp.where(n < max_exact, n, large)


def _block_diag_mean(n, group):
    i = jnp.arange(n)
    return jnp.where((i[:, None] // group) == (i[None, :] // group), 1.0 / group, 0.0).astype(BF16)


def kernel(x, meta_tokens, rel_bias, norm1_g, w_in, q_norm_g, k_norm_g, attn_sink, ret_decay,
           w_out, norm2_g, w_router, w_gate, w_up, w_down):
    B, seq, D = x.shape
    depth = w_in.shape[0]
    L = seq + N_META
    Lp = L + PAD_FRONT
    assert D == D_MODEL and Lp % BLOCK == 0
    cap = CAPACITY_FACTOR * L // N_EXPERTS
    cpad = -(-max(cap, WIDEST_WINDOW) // BF16_SUBLANES) * BF16_SUBLANES

    head = jnp.concatenate([jnp.zeros((PAD_FRONT, D), x.dtype), meta_tokens.astype(x.dtype)], axis=0)
    from_x = Lp // _row_tile(Lp) > 1
    hp = x if from_x else jnp.concatenate([jnp.broadcast_to(head[None], (B, HEAD_ROWS, D)), x], axis=1)

    pos = jnp.arange(Lp, dtype=F32)
    inv = ROPE_BASE ** (-jnp.arange(0, HEAD_DIM_R, 2, dtype=F32) / HEAD_DIM_R)
    ang = pos[:, None] * inv[None]
    cc = jnp.concatenate([jnp.cos(ang), jnp.cos(ang)], axis=-1)
    ss = jnp.concatenate([-jnp.sin(ang), jnp.sin(ang)], axis=-1)
    rel = (jnp.arange(3 * BLOCK)[None, :] - BLOCK) - jnp.arange(BLOCK)[:, None]
    bucket_onehot = (_t5_bucket(rel)[None] == jnp.arange(N_BUCKETS)[:, None, None]).astype(F32)
    bias = jnp.einsum('kqs,kh->hqs', bucket_onehot, rel_bias.astype(F32), precision=lax.Precision.HIGHEST)
    bias = jnp.where((jnp.abs(rel) <= BLOCK)[None], bias * LOG2E, NEG)
    key = jnp.arange(3 * BLOCK)
    bias = jnp.stack([jnp.where(key >= BLOCK + PAD_FRONT, bias, NEG), jnp.where(key >= PAD_FRONT, bias, NEG),
                      bias, jnp.where(key < 2 * BLOCK, bias, NEG)])
    bias = bias.reshape(4, N_KV_A, 2, 2, BLOCK, 3 * BLOCK).transpose(0, 1, 2, 4, 3, 5)
    bias = bias.reshape(4, N_KV_A, 2 * BLOCK, 6 * BLOCK)
    bdq = _block_diag_mean(WA_Q, HEAD_DIM_A)
    bdk = _block_diag_mean(WA_KV, HEAD_DIM_A)

    for l in range(depth):
        qg = (jnp.tile(q_norm_g[l].astype(F32), N_HEADS_A) * (HEAD_DIM_A ** -0.5 * LOG2E))[None]
        kg = jnp.tile(k_norm_g[l].astype(F32), N_KV_A)[None]
        lg2 = -jnp.exp(ret_decay[l].astype(F32))
        first = from_x and l == 0
        q, k, v, rq, rk, rv, gf, gb = _in_proj(hp, head, first, norm1_g[l][None].astype(F32),
                                               w_in[l].astype(BF16), cc, ss, qg, kg, bdq, bdk)
        sf, sb = _ret_states(lg2, rk, rv)
        wr = jnp.pad(w_router[l].astype(F32), ((0, 0), (0, LANES - N_EXPERTS)))
        wr_hi = wr.astype(BF16)
        wr_lo = (wr - wr_hi.astype(F32)).astype(BF16)
        wr_split = jnp.concatenate([wr_hi, wr_lo], axis=1)
        h1, hn, aff = _mixer(lg2, attn_sink[l].astype(F32) * LOG2E, q, k, v, rq, rk, rv, gf, gb, sf, sb, bias,
                             hp, head, first, w_out[l].astype(BF16), norm2_g[l][None].astype(F32), wr_split)
        posm, cpos = _route(aff, cap)
        cpos_flat = cpos.reshape(-1)
        xe = _gather(cpos_flat, hn, posm, cpad)
        ye = _ffn(xe, w_gate, w_up, w_down, l)
        last = l == depth - 1
        hp = _combine(cpos_flat, h1, posm, aff, ye, cpad, skip_rows=(PAD_FRONT + N_META) if last else 0)
    return hp
```

```python
import functools
import math

import jax
import jax.numpy as jnp
from jax import lax
from jax.experimental import pallas as pl
from jax.experimental.pallas import tpu as pltpu

D_MODEL = 1024
N_META = 16
BLOCK = 128
PAD_FRONT = BLOCK - N_META
EPS = 1e-6
N_HEADS_A = 8
N_KV_A = 2
GROUP_A = N_HEADS_A // N_KV_A
HEAD_DIM_A = 64
N_HEADS_R = 4
HEAD_DIM_R = 128
ROPE_BASE = 10000.0
N_BUCKETS = 32
MAX_DISTANCE = 128
N_EXPERTS = 16
CAPACITY_FACTOR = 2
WA_Q = N_HEADS_A * HEAD_DIM_A
WA_KV = N_KV_A * HEAD_DIM_A
WR = N_HEADS_R * HEAD_DIM_R
D_IN = WA_Q + 2 * WA_KV + 5 * WR
NEG = -1e30
LOG2E = math.log2(math.e)

LANES = 128
BF16_SUBLANES = 16
VMEM_LIMIT_CAP = 60000 * 1024

F32 = jnp.float32
BF16 = jnp.bfloat16
I32 = jnp.int32


def _params(semantics, vmem_bytes, **extra):
    return pltpu.CompilerParams(dimension_semantics=semantics,
                                vmem_limit_bytes=min(int(vmem_bytes), VMEM_LIMIT_CAP), **extra)


def _row_tile(lp):
    best = BLOCK
    for t in range(BLOCK, 1024 + 1, BLOCK):
        if lp % t == 0:
            best = t
    return best


HEAD_ROWS = PAD_FRONT + N_META


def _residual_tile(h_ref, head_ref, c, tm, from_x):
    blk = h_ref[0]
    if not from_x:
        return blk
    first = jnp.concatenate([head_ref[...], blk[:tm - HEAD_ROWS]], axis=0)
    return jnp.where(c == 0, first, blk)


def _residual_spec(tm, d, from_x):
    if not from_x:
        return pl.BlockSpec((1, tm, d), lambda b, c: (b, c, 0))
    start = lambda c: pl.multiple_of(jnp.maximum(c * tm - HEAD_ROWS, 0), BLOCK)
    return pl.BlockSpec((pl.Element(1), pl.Element(tm), pl.Element(d)), lambda b, c: (b, start(c), 0))


def _in_proj_kernel(h_ref, head_ref, g1_ref, w_ref, cc_ref, ss_ref, qg_ref, kg_ref, bdq_ref, bdk_ref,
                    q_ref, k_ref, v_ref, rq_ref, rk_ref, rv_ref, gf_ref, gb_ref, *, tm, from_x):
    j = pl.program_id(1)
    x = _residual_tile(h_ref, head_ref, j, tm, from_x)
    ms = jnp.mean(x * x, axis=-1, keepdims=True)
    xn = x * lax.rsqrt(ms + EPS) * g1_ref[...]
    row = j * tm + lax.broadcasted_iota(I32, (tm, 1), 0)
    xn = jnp.where(row >= PAD_FRONT, xn, 0.0).astype(BF16)

    def proj(lo, n):
        return jnp.dot(xn, w_ref[:, lo:lo + n], preferred_element_type=F32)

    q = proj(0, WA_Q)
    q_ms = jnp.dot((q * q).astype(BF16), bdq_ref[...], preferred_element_type=F32)
    q_ref[0] = (q * lax.rsqrt(q_ms + EPS) * qg_ref[...]).astype(BF16)
    k = proj(WA_Q, WA_KV)
    k_ms = jnp.dot((k * k).astype(BF16), bdk_ref[...], preferred_element_type=F32)
    k = k * lax.rsqrt(k_ms + EPS) * kg_ref[...]
    v = proj(WA_Q + WA_KV, WA_KV)
    lower = lax.broadcasted_iota(I32, (1, WA_KV), 1) < HEAD_DIM_A
    for t, ref in ((k, k_ref), (v, v_ref)):
        head0_lo = jnp.where(lower, t, 0.0)
        head1_hi = jnp.where(lower, 0.0, t)
        variants = (head0_lo, pltpu.roll(head0_lo, HEAD_DIM_A, 1), pltpu.roll(head1_hi, HEAD_DIM_A, 1), head1_hi)
        for i, var in enumerate(variants):
            ref[0, :, i * WA_KV:(i + 1) * WA_KV] = var.astype(BF16)

    cc = cc_ref[...]
    ss = ss_ref[...]
    base = WA_Q + 2 * WA_KV
    for off, ref, scale in ((base, rq_ref, 1.0), (base + WR, rk_ref, HEAD_DIM_R ** -0.5)):
        r = proj(off, WR)
        for hh in range(N_HEADS_R):
            seg = r[:, hh * HEAD_DIM_R:(hh + 1) * HEAD_DIM_R]
            rot = seg * cc + pltpu.roll(seg, HEAD_DIM_R // 2, 1) * ss
            ref[0, :, hh * HEAD_DIM_R:(hh + 1) * HEAD_DIM_R] = (rot * scale).astype(BF16)
    rv_ref[0] = proj(base + 2 * WR, WR).astype(BF16)
    gf_ref[0] = proj(base + 3 * WR, WR).astype(BF16)
    gb_ref[0] = proj(base + 4 * WR, WR).astype(BF16)


def _in_proj(h, head, from_x, g1, w_in_bf, cc, ss, qg, kg, bdq, bdk):
    B, D = h.shape[0], h.shape[2]
    Lp = cc.shape[0]
    tm = _row_tile(Lp)
    const = lambda b, j: (0, 0)
    row = lambda n: pl.BlockSpec((1, tm, n), lambda b, j: (b, j, 0))
    widths = (WA_Q, 2 * N_KV_A * WA_KV, 2 * N_KV_A * WA_KV, WR, WR, WR, WR, WR)
    return pl.pallas_call(
        functools.partial(_in_proj_kernel, tm=tm, from_x=from_x),
        grid=(B, Lp // tm),
        in_specs=[_residual_spec(tm, D, from_x),
                  pl.BlockSpec((HEAD_ROWS, D), const),
                  pl.BlockSpec((1, D), const),
                  pl.BlockSpec((D, D_IN), const),
                  pl.BlockSpec((tm, HEAD_DIM_R), lambda b, j: (j, 0)),
                  pl.BlockSpec((tm, HEAD_DIM_R), lambda b, j: (j, 0)),
                  pl.BlockSpec((1, WA_Q), const),
                  pl.BlockSpec((1, WA_KV), const),
                  pl.BlockSpec((WA_Q, WA_Q), const),
                  pl.BlockSpec((WA_KV, WA_KV), const)],
        out_specs=[row(n) for n in widths],
        out_shape=[jax.ShapeDtypeStruct((B, Lp, n), BF16) for n in widths],
        compiler_params=_params(("parallel", "parallel"), 48 << 20),
        name="in_proj",
    )(h, head, g1, w_in_bf, cc, ss, qg, kg, bdq, bdk)


def _ret_state_kernel(lg_ref, kf_ref, vf_ref, kb_ref, vb_ref, sf_ref, sb_ref, st_ref, *, nsub):
    t = pl.program_id(0)

    @pl.when(t == 0)
    def _():
        st_ref[...] = jnp.zeros_like(st_ref)

    idx = lax.broadcasted_iota(I32, (BLOCK, 1), 0).astype(F32)
    ones_row = jnp.ones((1, HEAD_DIM_R), F32)
    for d, (k_ref, v_ref, s_out) in enumerate(((kf_ref, vf_ref, sf_ref), (kb_ref, vb_ref, sb_ref))):
        for h in range(N_HEADS_R):
            lg = lg_ref[d, h]
            sl = slice(h * HEAD_DIM_R, (h + 1) * HEAD_DIM_R)
            zeta = jnp.exp(lg * ((BLOCK - 1.0 - idx) if d == 0 else idx))
            chunk_decay = jnp.exp(lg * float(BLOCK) * ones_row)
            for b in range(st_ref.shape[0]):
                st = st_ref[b, d, h]
                for step in range(nsub):
                    j = step if d == 0 else nsub - 1 - step
                    rows = slice(j * BLOCK, (j + 1) * BLOCK)
                    s_out[b, j, h] = st.astype(BF16)
                    kz = (k_ref[b, rows, sl].astype(F32) * zeta).astype(BF16)
                    kv = lax.dot_general(kz, v_ref[b, rows, sl], (((0,), (0,)), ((), ())),
                                         preferred_element_type=F32)
                    st = st * chunk_decay + kv
                st_ref[b, d, h] = st


def _ret_states(lg2, rk, rv):
    B, Lp, _ = rk.shape
    tm = _row_tile(Lp)
    nsub = tm // BLOCK
    nt = Lp // tm
    fwd = pl.BlockSpec((B, tm, WR), lambda t: (0, t, 0))
    bwd = pl.BlockSpec((B, tm, WR), lambda t: (0, nt - 1 - t, 0))
    st_shape = (B, nsub, N_HEADS_R, HEAD_DIM_R, HEAD_DIM_R)
    return pl.pallas_call(
        functools.partial(_ret_state_kernel, nsub=nsub),
        grid=(nt,),
        in_specs=[pl.BlockSpec(memory_space=pltpu.SMEM), fwd, fwd, bwd, bwd],
        out_specs=[pl.BlockSpec(st_shape, lambda t: (0, t, 0, 0, 0)),
                   pl.BlockSpec(st_shape, lambda t: (0, nt - 1 - t, 0, 0, 0))],
        out_shape=[jax.ShapeDtypeStruct((B, Lp // BLOCK) + st_shape[2:], BF16)] * 2,
        scratch_shapes=[pltpu.VMEM((B, 2, N_HEADS_R, HEAD_DIM_R, HEAD_DIM_R), F32)],
        compiler_params=_params(("arbitrary",), 48 << 20),
        name="ret_state",
    )(lg2, rk, rv, rk, rv)


def _mixer_kernel(lg_ref, sink_ref, q_ref, kp_ref, kc_ref, kn_ref, vp_ref, vc_ref, vn_ref,
                  rq_ref, rk_ref, rv_ref, gf_ref, gb_ref, sf_ref, sb_ref, bias_ref,
                  h_ref, head_ref, w_ref, g2_ref, wr_ref, h1_ref, hn_ref, aff_ref,
                  dec_ref, xi_ref, kbuf_ref, vbuf_ref, mix_ref, *, nblocks, tm, from_x):
    b = pl.program_id(0)
    c = pl.program_id(1)
    nsub = tm // BLOCK
    kv_w = 3 * BLOCK

    @pl.when((b == 0) & (c == 0))
    def _():
        ii = lax.broadcasted_iota(I32, (BLOCK, BLOCK), 0).astype(F32)
        jj = lax.broadcasted_iota(I32, (BLOCK, BLOCK), 1).astype(F32)
        for d in range(2):
            diff = (ii - jj) if d == 0 else (jj - ii)
            reach = (ii + 1.0) if d == 0 else (float(BLOCK) - ii)
            for h in range(N_HEADS_R):
                lg = lg_ref[d, h]
                dec_ref[d, h] = jnp.where(diff >= 0, jnp.exp(lg * jnp.maximum(diff, 0.0)), 0.0)
                xi_ref[d, h] = jnp.exp(lg * reach)

    for buf, (p_ref, c_ref, n_ref) in ((kbuf_ref, (kp_ref, kc_ref, kn_ref)), (vbuf_ref, (vp_ref, vc_ref, vn_ref))):
        buf[0:BLOCK] = p_ref[0]
        buf[BLOCK:BLOCK + tm] = c_ref[0]
        buf[BLOCK + tm:] = n_ref[0]

    lower_lanes = lax.broadcasted_iota(I32, (1, 2 * HEAD_DIM_A), 1) < HEAD_DIM_A
    upper_rows = lax.broadcasted_iota(I32, (2 * BLOCK, 1), 0) >= BLOCK

    def block(i, carry):
        r0 = pl.multiple_of(i * BLOCK, BLOCK)
        rows = pl.ds(r0, BLOCK)
        win = pl.ds(r0, kv_w)

        n = c * nsub + i
        variant = jnp.where(n < 2, n, jnp.where(n == nblocks - 1, 3, 2))
        for kh in range(N_KV_A):
            g0 = slice(2 * kh * WA_KV, (2 * kh + 1) * WA_KV)
            g1 = slice((2 * kh + 1) * WA_KV, (2 * kh + 2) * WA_KV)
            q2 = jnp.concatenate([q_ref[0, rows, g0], q_ref[0, rows, g1]], axis=0)
            kcat = jnp.concatenate([kbuf_ref[win, g0], kbuf_ref[win, g1]], axis=0)
            vcat = jnp.concatenate([vbuf_ref[win, g0], vbuf_ref[win, g1]], axis=0)
            s = lax.dot_general(q2, kcat, (((1,), (1,)), ((), ())), preferred_element_type=F32)
            s = s + bias_ref[variant, kh]
            probs = []
            denoms = []
            for half in range(2):
                sh = s[:, half * kv_w:(half + 1) * kv_w]
                sk = jnp.where(upper_rows, sink_ref[4 * kh + 2 + half], sink_ref[4 * kh + half])
                m = jnp.maximum(jnp.max(sh, axis=-1, keepdims=True), sk)
                p = jnp.exp2(sh - m)
                denoms.append(jnp.sum(p, axis=-1, keepdims=True) + jnp.exp2(sk - m))
                probs.append(p.astype(BF16))
            o = jnp.dot(jnp.concatenate(probs, axis=1), vcat, preferred_element_type=F32)
            o = o / jnp.where(lower_lanes, denoms[0], denoms[1])
            mix_ref[rows, g0] = o[:BLOCK].astype(BF16)
            mix_ref[rows, g1] = o[BLOCK:].astype(BF16)

        return carry

    def ret_block(i, carry):
        r0 = pl.multiple_of(i * BLOCK, BLOCK)
        rows = pl.ds(r0, BLOCK)
        for h in range(N_HEADS_R):
            sl = slice(h * HEAD_DIM_R, (h + 1) * HEAD_DIM_R)
            qh = rq_ref[0, rows, sl]
            vh = rv_ref[0, rows, sl]
            qk = lax.dot_general(qh, rk_ref[0, rows, sl], (((1,), (1,)), ((), ())),
                                 preferred_element_type=F32)
            qf = qh.astype(F32)
            acc = None
            for d, (st_ref, g_ref) in enumerate(((sf_ref, gf_ref), (sb_ref, gb_ref))):
                lhs = jnp.concatenate([(qk * dec_ref[d, h]).astype(BF16), (qf * xi_ref[d, h]).astype(BF16)], axis=1)
                rhs = jnp.concatenate([vh, st_ref[0, i, h]], axis=0)
                y = jnp.dot(lhs, rhs, preferred_element_type=F32)
                mu = jnp.mean(y, axis=-1, keepdims=True)
                yc = y - mu
                var = jnp.mean(yc * yc, axis=-1, keepdims=True)
                gate = g_ref[0, rows, sl].astype(F32)
                term = gate * jax.nn.sigmoid(gate) * (yc * lax.rsqrt(var + EPS))
                acc = term if acc is None else acc + term
            mix_ref[rows, WA_Q + h * HEAD_DIM_R:WA_Q + (h + 1) * HEAD_DIM_R] = acc.astype(BF16)
        return carry

    for i in range(nsub):
        block(i, 0)
    for i in range(nsub):
        ret_block(i, 0)

    h1 = (_residual_tile(h_ref, head_ref, c, tm, from_x)
          + jnp.dot(mix_ref[...], w_ref[...], preferred_element_type=F32))
    h1_ref[0] = h1
    ms = jnp.mean(h1 * h1, axis=-1, keepdims=True)
    xn = h1 * lax.rsqrt(ms + EPS) * g2_ref[...]
    hn_ref[0] = xn.astype(BF16)
    hi = xn.astype(BF16)
    lo = (xn - hi.astype(F32)).astype(BF16)
    both = jnp.dot(hi, wr_ref[...], preferred_element_type=F32)
    logits = (both[:, :LANES] + both[:, LANES:]
              + jnp.dot(lo, wr_ref[:, :LANES], preferred_element_type=F32))
    lt = logits.T[:N_EXPERTS]
    mx = jnp.max(lt, axis=0, keepdims=True)
    ex = jnp.exp(lt - mx)
    aff = ex / jnp.sum(ex, axis=0, keepdims=True)
    col = c * tm + lax.broadcasted_iota(I32, (1, tm), 1)
    aff_ref[0] = jnp.where(col >= PAD_FRONT, aff, -1.0)


def _mixer(lg2, sink, q, k4, v4, rq, rk, rv, gf, gb, sf, sb, bias2, h, head, from_x, w_out_bf, g2, wr_split):
    B, Lp, _ = q.shape
    D = h.shape[2]
    nb = Lp // BLOCK
    assert nb >= 3
    tm = _row_tile(Lp)
    nsub = tm // BLOCK
    kvw = k4.shape[-1]
    const = lambda b, c: (0, 0)
    cur = lambda n_: pl.BlockSpec((1, tm, n_), lambda b, c: (b, c, 0))
    prev = pl.BlockSpec((1, BLOCK, kvw), lambda b, c: (b, jnp.maximum(c * nsub - 1, 0), 0))
    nxt = pl.BlockSpec((1, BLOCK, kvw), lambda b, c: (b, jnp.minimum((c + 1) * nsub, nb - 1), 0))
    st = pl.BlockSpec((1, nsub, N_HEADS_R, HEAD_DIM_R, HEAD_DIM_R), lambda b, c: (b, c, 0, 0, 0))
    smem = pl.BlockSpec(memory_space=pltpu.SMEM)
    tab = pltpu.VMEM((2, N_HEADS_R, BLOCK, HEAD_DIM_R), F32)
    kvbuf = pltpu.VMEM((tm + 2 * BLOCK, kvw), BF16)
    return pl.pallas_call(
        functools.partial(_mixer_kernel, nblocks=nb, tm=tm, from_x=from_x),
        grid=(B, Lp // tm),
        in_specs=[smem, smem, cur(WA_Q), prev, cur(kvw), nxt, prev, cur(kvw), nxt,
                  cur(WR), cur(WR), cur(WR), cur(WR), cur(WR), st, st,
                  pl.BlockSpec(bias2.shape, lambda b, c: (0, 0, 0, 0)),
                  _residual_spec(tm, D, from_x), pl.BlockSpec((HEAD_ROWS, D), const),
                  pl.BlockSpec((D, D), const), pl.BlockSpec((1, D), const),
                  pl.BlockSpec((D, 2 * LANES), const)],
        out_specs=[cur(D), cur(D), pl.BlockSpec((1, N_EXPERTS, tm), lambda b, c: (b, 0, c))],
        out_shape=[jax.ShapeDtypeStruct((B, Lp, D), F32),
                   jax.ShapeDtypeStruct((B, Lp, D), BF16),
                   jax.ShapeDtypeStruct((B, N_EXPERTS, Lp), F32)],
        scratch_shapes=[tab, tab, kvbuf, kvbuf, pltpu.VMEM((tm, D), BF16)],
        compiler_params=_params(("arbitrary", "arbitrary"), 56 << 20),
        name="mixer",
    )(lg2, sink, q, k4, k4, k4, v4, v4, v4, rq, rk, rv, gf, gb, sf, sb, bias2, h, head, w_out_bf, g2, wr_split)


def _route_kernel(aff_ref, posm_ref, cpos_ref, *, cap, nb):
    E = aff_ref.shape[0]

    def keys(sl=slice(None)):
        return pltpu.bitcast(aff_ref[:, sl], I32)

    def count_ge(t):
        return jnp.sum((keys() >= t).astype(I32), axis=-1, keepdims=True)

    def search(_, carry):
        lo, hi = carry
        mid = lo + (hi - lo) // 2
        ok = count_ge(mid) >= cap
        return jnp.where(ok, mid, lo), jnp.where(ok, hi, mid)

    lo0 = jnp.zeros((E, 1), I32)
    hi0 = jnp.full((E, 1), 0x3F800001, I32)
    thr, _ = lax.fori_loop(0, 31, search, (lo0, hi0))
    n_gt = jnp.sum((keys() > thr).astype(F32), axis=-1, keepdims=True)
    need_eq = float(cap) - n_gt

    jr = lax.broadcasted_iota(I32, (LANES, LANES), 0)
    jc = lax.broadcasted_iota(I32, (LANES, LANES), 1)
    upper = (jr < jc).astype(F32).astype(BF16)
    lane = lax.broadcasted_iota(I32, (E, LANES), 1)

    def chunk(c, carry):
        cg, ce, cp = carry
        off = pl.multiple_of(c * LANES, LANES)
        kk = pltpu.bitcast(aff_ref[:, pl.ds(off, LANES)], I32)
        gt = kk > thr
        eq = kk == thr
        both = jnp.concatenate([gt, eq], axis=0).astype(F32)
        ex = jnp.dot(both.astype(BF16), upper, preferred_element_type=F32)
        exg = ex[:E] + cg
        exe = ex[E:] + ce
        sel = gt | (eq & (exe < need_eq))
        pos = exg + jnp.minimum(exe, need_eq)
        posm_ref[:, pl.ds(off, LANES)] = jnp.where(sel, pos, -1.0).astype(I32)
        cp = jnp.where(lane == c, (cg + jnp.minimum(ce, need_eq)).astype(I32), cp)
        tot = jnp.sum(both, axis=-1, keepdims=True)
        return cg + tot[:E], ce + tot[E:], cp

    zero = jnp.zeros((E, 1), F32)
    cg, ce, cp = lax.fori_loop(0, nb, chunk, (zero, zero, jnp.zeros((E, LANES), I32)))
    cpos_ref[...] = jnp.where(lane == nb, (cg + jnp.minimum(ce, need_eq)).astype(I32), cp)


def _route(aff_t, cap):
    B, E, Lp = aff_t.shape
    nb = Lp // LANES
    assert nb + 1 <= LANES
    posm, cpos = pl.pallas_call(
        functools.partial(_route_kernel, cap=cap, nb=nb),
        grid=(1,),
        in_specs=[pl.BlockSpec((B * E, Lp), lambda i: (0, 0))],
        out_specs=[pl.BlockSpec((B * E, Lp), lambda i: (0, 0)),
                   pl.BlockSpec((B * E, LANES), lambda i: (0, 0))],
        out_shape=[jax.ShapeDtypeStruct((B * E, Lp), I32),
                   jax.ShapeDtypeStruct((B * E, LANES), I32)],
        compiler_params=_params(("arbitrary",), 32 << 20),
        name="route",
    )(aff_t.reshape(B * E, Lp))
    return posm.reshape(B, E, Lp), cpos


NARROW_WINDOW = {1: 3 * BF16_SUBLANES, 2: 4 * BF16_SUBLANES}
MAX_CHUNK_BLOCKS = 2
WIDEST_WINDOW = MAX_CHUNK_BLOCKS * BLOCK + BF16_SUBLANES


def _wide_window(nblk):
    return nblk * BLOCK + BF16_SUBLANES


def _chunks(nsub):
    out = []
    i = 0
    while i < nsub:
        n = min(MAX_CHUNK_BLOCKS, nsub - i)
        out.append((i, n))
        i += n
    return out


def _window(cpos_ref, b, e, c, nblk, w, cpad):
    base = (b * N_EXPERTS + e) * LANES + c
    a0 = jnp.minimum((cpos_ref[base] // BF16_SUBLANES) * BF16_SUBLANES, cpad - w)
    return pl.multiple_of(a0, BF16_SUBLANES), cpos_ref[base + nblk] - a0 <= w


def _windows(cpos_ref, b, experts, c, nblk, w, cpad):
    starts = []
    fits = None
    for e in experts:
        a0, ok = _window(cpos_ref, b, e, c, nblk, w, cpad)
        starts.append(a0)
        fits = ok if fits is None else fits & ok
    return starts, fits


EXPERT_GROUP = 8


def _gather_kernel(cpos_ref, hn_ref, posm_ref, x_ref, *, nsub, cpad):
    b = pl.program_id(0)
    g = pl.program_id(1)
    co = pl.program_id(2)
    experts = [g * EXPERT_GROUP + e for e in range(EXPERT_GROUP)]

    @pl.when(co == 0)
    def _():
        x_ref[...] = jnp.zeros_like(x_ref)

    def move(c, nblk, tokens, w, starts=None):
        if starts is None:
            starts = _windows(cpos_ref, b, experts, c, nblk, w, cpad)[0]
        rows = lax.broadcasted_iota(I32, (w, nblk * BLOCK), 0)
        onehots = []
        for e in range(EXPERT_GROUP):
            hit = rows + starts[e] == posm_ref[0, 0, e:e + 1, tokens]
            onehots.append(hit.astype(F32).astype(BF16))
        res = jnp.dot(jnp.concatenate(onehots, axis=0), hn_ref[0, tokens, :],
                      preferred_element_type=F32).astype(BF16)
        for e in range(EXPERT_GROUP):
            x_ref[0, e, pl.ds(starts[e], w), :] += res[e * w:(e + 1) * w]

    for first, nblk in _chunks(nsub):
        c = co * nsub + first
        tokens = slice(first * BLOCK, (first + nblk) * BLOCK)
        narrow_w = NARROW_WINDOW[nblk]
        narrow_starts, narrow = _windows(cpos_ref, b, experts, c, nblk, narrow_w, cpad)
        pl.when(narrow)(functools.partial(move, c, nblk, tokens, narrow_w, narrow_starts))
        pl.when(jnp.logical_not(narrow))(functools.partial(move, c, nblk, tokens, _wide_window(nblk)))


def _gather(cpos_flat, hn, posm, cpad):
    B, Lp, D = hn.shape
    E = N_EXPERTS
    ng = E // EXPERT_GROUP
    tm = _row_tile(Lp)
    posm4 = posm.reshape(B, ng, EXPERT_GROUP, Lp)
    grid_spec = pltpu.PrefetchScalarGridSpec(
        num_scalar_prefetch=1,
        grid=(B, ng, Lp // tm),
        in_specs=[pl.BlockSpec((1, tm, D), lambda b, g, c, cp: (b, c, 0)),
                  pl.BlockSpec((1, 1, EXPERT_GROUP, tm), lambda b, g, c, cp: (b, g, 0, c))],
        out_specs=pl.BlockSpec((1, EXPERT_GROUP, cpad, D), lambda b, g, c, cp: (b, g, 0, 0)),
    )
    return pl.pallas_call(
        functools.partial(_gather_kernel, nsub=tm // BLOCK, cpad=cpad),
        grid_spec=grid_spec,
        out_shape=jax.ShapeDtypeStruct((B, E, cpad, D), BF16),
        compiler_params=_params(("arbitrary", "arbitrary", "arbitrary"), 48 << 20),
        name="gather",
    )(cpos_flat, hn, posm4)


FFN_TILE = 256
FFN_BATCH = 2


def _ffn_kernel(x_ref, wg_ref, wu_ref, wd_ref, y_ref, hm_ref, wd_bf_ref, *, tf):
    f = pl.program_id(2)
    bh, _, cpad, _ = x_ref.shape
    wg = wg_ref[0, 0].astype(BF16)
    wu = wu_ref[0, 0].astype(BF16)
    off = pl.multiple_of(f * tf, tf)
    for i in range(bh):
        x = x_ref[i, 0]
        a = jnp.dot(x, wg, preferred_element_type=F32)
        u = jnp.dot(x, wu, preferred_element_type=F32)
        hm_ref[i * cpad:(i + 1) * cpad, pl.ds(off, tf)] = (a * jax.nn.sigmoid(a) * u).astype(BF16)
    wd_bf_ref[pl.ds(off, tf), :] = wd_ref[0, 0].astype(BF16)

    @pl.when(f == pl.num_programs(2) - 1)
    def _():
        for i in range(bh):
            y_ref[i, 0] = jnp.dot(hm_ref[i * cpad:(i + 1) * cpad, :], wd_bf_ref[...],
                                  preferred_element_type=F32).astype(BF16)


def _ffn(xe, w_gate, w_up, w_down, layer):
    B, E, cpad, D = xe.shape
    F = w_gate.shape[-1]
    tf = FFN_TILE if F % FFN_TILE == 0 else LANES
    bh = FFN_BATCH if B % FFN_BATCH == 0 else 1
    xspec = pl.BlockSpec((bh, 1, cpad, D), lambda e, b, f: (b, e, 0, 0))
    return pl.pallas_call(
        functools.partial(_ffn_kernel, tf=tf),
        grid=(E, B // bh, F // tf),
        in_specs=[xspec,
                  pl.BlockSpec((1, 1, D, tf), lambda e, b, f: (layer, e, 0, f)),
                  pl.BlockSpec((1, 1, D, tf), lambda e, b, f: (layer, e, 0, f)),
                  pl.BlockSpec((1, 1, tf, D), lambda e, b, f: (layer, e, f, 0))],
        out_specs=xspec,
        out_shape=jax.ShapeDtypeStruct(xe.shape, BF16),
        scratch_shapes=[pltpu.VMEM((bh * cpad, F), BF16), pltpu.VMEM((F, D), BF16)],
        compiler_params=_params(("parallel", "parallel", "arbitrary"), 56 << 20),
        name="ffn",
    )(xe, w_gate, w_up, w_down)


def _combine_kernel(cpos_ref, h_ref, posm_ref, gate_ref, ye_ref, o_ref, *, nsub, chunk0, cpad):
    b = pl.program_id(0)
    co = pl.program_id(1)
    experts = list(range(N_EXPERTS))

    def move(c, nblk, tokens, w, starts=None):
        if starts is None:
            starts = _windows(cpos_ref, b, experts, c, nblk, w, cpad)[0]
        rows = lax.broadcasted_iota(I32, (w, nblk * BLOCK), 0)
        weights = []
        windows = []
        for e in experts:
            hit = rows + starts[e] == posm_ref[0, e:e + 1, tokens]
            weights.append(jnp.where(hit, gate_ref[0, e:e + 1, tokens], 0.0).astype(BF16))
            windows.append(ye_ref[0, e, pl.ds(starts[e], w), :])
        moe = lax.dot_general(jnp.concatenate(weights, axis=0), jnp.concatenate(windows, axis=0),
                              (((0,), (0,)), ((), ())), preferred_element_type=F32)
        o_ref[0, tokens, :] = h_ref[0, tokens, :] + moe

    for first, nblk in _chunks(nsub):
        c = chunk0 + co * nsub + first
        tokens = slice(first * BLOCK, (first + nblk) * BLOCK)
        narrow_w = NARROW_WINDOW[nblk]
        narrow_starts, narrow = _windows(cpos_ref, b, experts, c, nblk, narrow_w, cpad)
        pl.when(narrow)(functools.partial(move, c, nblk, tokens, narrow_w, narrow_starts))
        pl.when(jnp.logical_not(narrow))(functools.partial(move, c, nblk, tokens, _wide_window(nblk)))


def _combine(cpos_flat, h1, posm, gate, ye, cpad, skip_rows=0):
    B, Lp, D = h1.shape
    E = N_EXPERTS
    rows_out = Lp - skip_rows
    tm = _row_tile(rows_out)
    assert skip_rows % BLOCK == 0
    el = lambda *dims: tuple(pl.Element(d) for d in dims)
    first = lambda c: pl.multiple_of(skip_rows + c * tm, BLOCK)
    grid_spec = pltpu.PrefetchScalarGridSpec(
        num_scalar_prefetch=1,
        grid=(B, rows_out // tm),
        in_specs=[pl.BlockSpec(el(1, tm, D), lambda b, c, cp: (b, first(c), 0)),
                  pl.BlockSpec(el(1, E, tm), lambda b, c, cp: (b, 0, first(c))),
                  pl.BlockSpec(el(1, E, tm), lambda b, c, cp: (b, 0, first(c))),
                  pl.BlockSpec((1, E, cpad, D), lambda b, c, cp: (b, 0, 0, 0),
                               pipeline_mode=pl.Buffered(1))],
        out_specs=pl.BlockSpec((1, tm, D), lambda b, c, cp: (b, c, 0)),
    )
    return pl.pallas_call(
        functools.partial(_combine_kernel, nsub=tm // BLOCK, chunk0=skip_rows // BLOCK, cpad=cpad),
        grid_spec=grid_spec,
        out_shape=jax.ShapeDtypeStruct((B, rows_out, D), F32),
        compiler_params=_params(("arbitrary", "arbitrary"), VMEM_LIMIT_CAP),
        name="combine",
    )(cpos_flat, h1, posm, gate, ye)


def _t5_bucket(rel):
    half = N_BUCKETS // 2
    max_exact = half // 2
    n = jnp.abs(rel)
    large = max_exact + (jnp.log(jnp.maximum(n, max_exact).astype(F32) / max_exact)
                         / math.log(MAX_DISTANCE / max_exact) * (half - max_exact)).astype(I32)
    large = jnp.minimum(large, half - 1)
    return jnp.where(rel > 0, half, 0) + jnp.where(n < max_exact, n, large)


def _block_diag_mean(n, group):
    i = jnp.arange(n)
    return jnp.where((i[:, None] // group) == (i[None, :] // group), 1.0 / group, 0.0).astype(BF16)


def kernel(x, meta_tokens, rel_bias, norm1_g, w_in, q_norm_g, k_norm_g, attn_sink, ret_decay,
           w_out, norm2_g, w_router, w_gate, w_up, w_down):
    B, seq, D = x.shape
    depth = w_in.shape[0]
    L = seq + N_META
    Lp = L + PAD_FRONT
    assert D == D_MODEL and Lp % BLOCK == 0
    cap = CAPACITY_FACTOR * L // N_EXPERTS
    cpad = -(-max(cap, WIDEST_WINDOW) // BF16_SUBLANES) * BF16_SUBLANES

    head = jnp.concatenate([jnp.zeros((PAD_FRONT, D), x.dtype), meta_tokens.astype(x.dtype)], axis=0)
    from_x = Lp // _row_tile(Lp) > 1
    hp = x if from_x else jnp.concatenate([jnp.broadcast_to(head[None], (B, HEAD_ROWS, D)), x], axis=1)

    pos = jnp.arange(Lp, dtype=F32)
    inv = ROPE_BASE ** (-jnp.arange(0, HEAD_DIM_R, 2, dtype=F32) / HEAD_DIM_R)
    ang = pos[:, None] * inv[None]
    cc = jnp.concatenate([jnp.cos(ang), jnp.cos(ang)], axis=-1)
    ss = jnp.concatenate([-jnp.sin(ang), jnp.sin(ang)], axis=-1)
    rel = (jnp.arange(3 * BLOCK)[None, :] - BLOCK) - jnp.arange(BLOCK)[:, None]
    bucket_onehot = (_t5_bucket(rel)[None] == jnp.arange(N_BUCKETS)[:, None, None]).astype(F32)
    bias = jnp.einsum('kqs,kh->hqs', bucket_onehot, rel_bias.astype(F32), precision=lax.Precision.HIGHEST)
    bias = jnp.where((jnp.abs(rel) <= BLOCK)[None], bias * LOG2E, NEG)
    key = jnp.arange(3 * BLOCK)
    bias = jnp.stack([jnp.where(key >= BLOCK + PAD_FRONT, bias, NEG), jnp.where(key >= PAD_FRONT, bias, NEG),
                      bias, jnp.where(key < 2 * BLOCK, bias, NEG)])
    bias = bias.reshape(4, N_KV_A, 2, 2, BLOCK, 3 * BLOCK).transpose(0, 1, 2, 4, 3, 5)
    bias = bias.reshape(4, N_KV_A, 2 * BLOCK, 6 * BLOCK)
    bdq = _block_diag_mean(WA_Q, HEAD_DIM_A)
    bdk = _block_diag_mean(WA_KV, HEAD_DIM_A)

    for l in range(depth):
        qg = (jnp.tile(q_norm_g[l].astype(F32), N_HEADS_A) * (HEAD_DIM_A ** -0.5 * LOG2E))[None]
        kg = jnp.tile(k_norm_g[l].astype(F32), N_KV_A)[None]
        lg2 = -jnp.exp(ret_decay[l].astype(F32))
        first = from_x and l == 0
        q, k, v, rq, rk, rv, gf, gb = _in_proj(hp, head, first, norm1_g[l][None].astype(F32),
                                               w_in[l].astype(BF16), cc, ss, qg, kg, bdq, bdk)
        sf, sb = _ret_states(lg2, rk, rv)
        wr = jnp.pad(w_router[l].astype(F32), ((0, 0), (0, LANES - N_EXPERTS)))
        wr_hi = wr.astype(BF16)
        wr_lo = (wr - wr_hi.astype(F32)).astype(BF16)
        wr_split = jnp.concatenate([wr_hi, wr_lo], axis=1)
        h1, hn, aff = _mixer(lg2, attn_sink[l].astype(F32) * LOG2E, q, k, v, rq, rk, rv, gf, gb, sf, sb, bias,
                             hp, head, first, w_out[l].astype(BF16), norm2_g[l][None].astype(F32), wr_split)
        posm, cpos = _route(aff, cap)
        cpos_flat = cpos.reshape(-1)
        xe = _gather(cpos_flat, hn, posm, cpad)
        ye = _ffn(xe, w_gate, w_up, w_down, l)
        last = l == depth - 1
        hp = _combine(cpos_flat, h1, posm, aff, ye, cpad, skip_rows=(PAD_FRONT + N_META) if last else 0)
    return hp
```

```python
import functools
import math

import jax
import jax.numpy as jnp
from jax import lax
from jax.experimental import pallas as pl
from jax.experimental.pallas import tpu as pltpu

D_MODEL = 1024
N_META = 16
BLOCK = 128
PAD_FRONT = BLOCK - N_META
EPS = 1e-6
N_HEADS_A = 8
N_KV_A = 2
GROUP_A = N_HEADS_A // N_KV_A
HEAD_DIM_A = 64
N_HEADS_R = 4
HEAD_DIM_R = 128
ROPE_BASE = 10000.0
N_BUCKETS = 32
MAX_DISTANCE = 128
N_EXPERTS = 16
CAPACITY_FACTOR = 2
WA_Q = N_HEADS_A * HEAD_DIM_A
WA_KV = N_KV_A * HEAD_DIM_A
WR = N_HEADS_R * HEAD_DIM_R
D_IN = WA_Q + 2 * WA_KV + 5 * WR
NEG = -1e30
LOG2E = math.log2(math.e)

LANES = 128
BF16_SUBLANES = 16
VMEM_LIMIT_CAP = 60000 * 1024

F32 = jnp.float32
BF16 = jnp.bfloat16
I32 = jnp.int32


def _params(semantics, vmem_bytes, **extra):
    return pltpu.CompilerParams(dimension_semantics=semantics,
                                vmem_limit_bytes=min(int(vmem_bytes), VMEM_LIMIT_CAP), **extra)


def _row_tile(lp, limit=1024):
    best = BLOCK
    for t in range(BLOCK, limit + 1, BLOCK):
        if lp % t == 0:
            best = t
    return best


HEAD_ROWS = PAD_FRONT + N_META


def _residual_tile(h_ref, head_ref, c, tm, from_x):
    blk = h_ref[0]
    if not from_x:
        return blk
    first = jnp.concatenate([head_ref[...], blk[:tm - HEAD_ROWS]], axis=0)
    return jnp.where(c == 0, first, blk)


def _residual_spec(tm, d, from_x):
    if not from_x:
        return pl.BlockSpec((1, tm, d), lambda b, c: (b, c, 0))
    start = lambda c: pl.multiple_of(jnp.maximum(c * tm - HEAD_ROWS, 0), BLOCK)
    return pl.BlockSpec((pl.Element(1), pl.Element(tm), pl.Element(d)), lambda b, c: (b, start(c), 0))


def _in_proj_kernel(h_ref, head_ref, g1_ref, w_ref, cc_ref, ss_ref, qg_ref, kg_ref, bdq_ref, bdk_ref,
                    q_ref, k_ref, v_ref, rq_ref, rk_ref, rv_ref, gf_ref, gb_ref, *, tm, from_x):
    j = pl.program_id(1)
    x = _residual_tile(h_ref, head_ref, j, tm, from_x)
    ms = jnp.mean(x * x, axis=-1, keepdims=True)
    xn = x * lax.rsqrt(ms + EPS) * g1_ref[...]
    row = j * tm + lax.broadcasted_iota(I32, (tm, 1), 0)
    xn = jnp.where(row >= PAD_FRONT, xn, 0.0).astype(BF16)

    def proj(lo, n):
        return jnp.dot(xn, w_ref[:, lo:lo + n], preferred_element_type=F32)

    q = proj(0, WA_Q)
    q_ms = jnp.dot((q * q).astype(BF16), bdq_ref[...], preferred_element_type=F32)
    q_ref[0] = (q * lax.rsqrt(q_ms + EPS) * qg_ref[...]).astype(BF16)
    k = proj(WA_Q, WA_KV)
    k_ms = jnp.dot((k * k).astype(BF16), bdk_ref[...], preferred_element_type=F32)
    k = k * lax.rsqrt(k_ms + EPS) * kg_ref[...]
    v = proj(WA_Q + WA_KV, WA_KV)
    lower = lax.broadcasted_iota(I32, (1, WA_KV), 1) < HEAD_DIM_A
    for t, ref in ((k, k_ref), (v, v_ref)):
        head0_lo = jnp.where(lower, t, 0.0)
        head1_hi = jnp.where(lower, 0.0, t)
        variants = (head0_lo, pltpu.roll(head0_lo, HEAD_DIM_A, 1), pltpu.roll(head1_hi, HEAD_DIM_A, 1), head1_hi)
        for i, var in enumerate(variants):
            ref[0, :, i * WA_KV:(i + 1) * WA_KV] = var.astype(BF16)

    cc = cc_ref[...]
    ss = ss_ref[...]
    base = WA_Q + 2 * WA_KV
    for off, ref, scale in ((base, rq_ref, 1.0), (base + WR, rk_ref, HEAD_DIM_R ** -0.5)):
        r = proj(off, WR)
        for hh in range(N_HEADS_R):
            seg = r[:, hh * HEAD_DIM_R:(hh + 1) * HEAD_DIM_R]
            rot = seg * cc + pltpu.roll(seg, HEAD_DIM_R // 2, 1) * ss
            ref[0, :, hh * HEAD_DIM_R:(hh + 1) * HEAD_DIM_R] = (rot * scale).astype(BF16)
    rv_ref[0] = proj(base + 2 * WR, WR).astype(BF16)
    gf_ref[0] = proj(base + 3 * WR, WR).astype(BF16)
    gb_ref[0] = proj(base + 4 * WR, WR).astype(BF16)


def _in_proj(h, head, from_x, g1, w_in_bf, cc, ss, qg, kg, bdq, bdk):
    B, D = h.shape[0], h.shape[2]
    Lp = cc.shape[0]
    tm = _row_tile(Lp)
    const = lambda b, j: (0, 0)
    row = lambda n: pl.BlockSpec((1, tm, n), lambda b, j: (b, j, 0))
    widths = (WA_Q, 2 * N_KV_A * WA_KV, 2 * N_KV_A * WA_KV, WR, WR, WR, WR, WR)
    return pl.pallas_call(
        functools.partial(_in_proj_kernel, tm=tm, from_x=from_x),
        grid=(B, Lp // tm),
        in_specs=[_residual_spec(tm, D, from_x),
                  pl.BlockSpec((HEAD_ROWS, D), const),
                  pl.BlockSpec((1, D), const),
                  pl.BlockSpec((D, D_IN), const),
                  pl.BlockSpec((tm, HEAD_DIM_R), lambda b, j: (j, 0)),
                  pl.BlockSpec((tm, HEAD_DIM_R), lambda b, j: (j, 0)),
                  pl.BlockSpec((1, WA_Q), const),
                  pl.BlockSpec((1, WA_KV), const),
                  pl.BlockSpec((WA_Q, WA_Q), const),
                  pl.BlockSpec((WA_KV, WA_KV), const)],
        out_specs=[row(n) for n in widths],
        out_shape=[jax.ShapeDtypeStruct((B, Lp, n), BF16) for n in widths],
        compiler_params=_params(("parallel", "parallel"), 48 << 20),
        name="in_proj",
    )(h, head, g1, w_in_bf, cc, ss, qg, kg, bdq, bdk)


def _ret_state_kernel(lg_ref, kf_ref, vf_ref, kb_ref, vb_ref, sf_ref, sb_ref, st_ref, *, nsub):
    t = pl.program_id(0)

    @pl.when(t == 0)
    def _():
        st_ref[...] = jnp.zeros_like(st_ref)

    idx = lax.broadcasted_iota(I32, (BLOCK, 1), 0).astype(F32)
    ones_row = jnp.ones((1, HEAD_DIM_R), F32)
    for d, (k_ref, v_ref, s_out) in enumerate(((kf_ref, vf_ref, sf_ref), (kb_ref, vb_ref, sb_ref))):
        for h in range(N_HEADS_R):
            lg = lg_ref[d, h]
            sl = slice(h * HEAD_DIM_R, (h + 1) * HEAD_DIM_R)
            zeta = jnp.exp(lg * ((BLOCK - 1.0 - idx) if d == 0 else idx))
            chunk_decay = jnp.exp(lg * float(BLOCK) * ones_row)
            for b in range(st_ref.shape[0]):
                st = st_ref[b, d, h]
                for step in range(nsub):
                    j = step if d == 0 else nsub - 1 - step
                    rows = slice(j * BLOCK, (j + 1) * BLOCK)
                    s_out[b, j, h] = st.astype(BF16)
                    kz = (k_ref[b, rows, sl].astype(F32) * zeta).astype(BF16)
                    kv = lax.dot_general(kz, v_ref[b, rows, sl], (((0,), (0,)), ((), ())),
                                         preferred_element_type=F32)
                    st = st * chunk_decay + kv
                st_ref[b, d, h] = st


def _ret_states(lg2, rk, rv):
    B, Lp, _ = rk.shape
    tm = _row_tile(Lp)
    nsub = tm // BLOCK
    nt = Lp // tm
    fwd = pl.BlockSpec((B, tm, WR), lambda t: (0, t, 0))
    bwd = pl.BlockSpec((B, tm, WR), lambda t: (0, nt - 1 - t, 0))
    st_shape = (B, nsub, N_HEADS_R, HEAD_DIM_R, HEAD_DIM_R)
    return pl.pallas_call(
        functools.partial(_ret_state_kernel, nsub=nsub),
        grid=(nt,),
        in_specs=[pl.BlockSpec(memory_space=pltpu.SMEM), fwd, fwd, bwd, bwd],
        out_specs=[pl.BlockSpec(st_shape, lambda t: (0, t, 0, 0, 0)),
                   pl.BlockSpec(st_shape, lambda t: (0, nt - 1 - t, 0, 0, 0))],
        out_shape=[jax.ShapeDtypeStruct((B, Lp // BLOCK) + st_shape[2:], BF16)] * 2,
        scratch_shapes=[pltpu.VMEM((B, 2, N_HEADS_R, HEAD_DIM_R, HEAD_DIM_R), F32)],
        compiler_params=_params(("arbitrary",), 48 << 20),
        name="ret_state",
    )(lg2, rk, rv, rk, rv)


def _mixer_kernel(lg_ref, sink_ref, q_ref, kp_ref, kc_ref, kn_ref, vp_ref, vc_ref, vn_ref,
                  rq_ref, rk_ref, rv_ref, gf_ref, gb_ref, sf_ref, sb_ref, bias_ref,
                  h_ref, head_ref, w_ref, g2_ref, wr_ref, h1_ref, hn_ref, aff_ref,
                  dec_ref, xi_ref, kbuf_ref, vbuf_ref, mix_ref, *, nblocks, tm, from_x):
    b = pl.program_id(0)
    c = pl.program_id(1)
    nsub = tm // BLOCK
    kv_w = 3 * BLOCK

    @pl.when((b == 0) & (c == 0))
    def _():
        ii = lax.broadcasted_iota(I32, (BLOCK, BLOCK), 0).astype(F32)
        jj = lax.broadcasted_iota(I32, (BLOCK, BLOCK), 1).astype(F32)
        for d in range(2):
            diff = (ii - jj) if d == 0 else (jj - ii)
            reach = (ii + 1.0) if d == 0 else (float(BLOCK) - ii)
            for h in range(N_HEADS_R):
                lg = lg_ref[d, h]
                dec_ref[d, h] = jnp.where(diff >= 0, jnp.exp(lg * jnp.maximum(diff, 0.0)), 0.0)
                xi_ref[d, h] = jnp.exp(lg * reach)

    for buf, (p_ref, c_ref, n_ref) in ((kbuf_ref, (kp_ref, kc_ref, kn_ref)), (vbuf_ref, (vp_ref, vc_ref, vn_ref))):
        buf[0:BLOCK] = p_ref[0]
        buf[BLOCK:BLOCK + tm] = c_ref[0]
        buf[BLOCK + tm:] = n_ref[0]

    lower_lanes = lax.broadcasted_iota(I32, (1, 2 * HEAD_DIM_A), 1) < HEAD_DIM_A
    upper_rows = lax.broadcasted_iota(I32, (2 * BLOCK, 1), 0) >= BLOCK

    def block(i, carry):
        r0 = pl.multiple_of(i * BLOCK, BLOCK)
        rows = pl.ds(r0, BLOCK)
        win = pl.ds(r0, kv_w)

        n = c * nsub + i
        variant = jnp.where(n < 2, n, jnp.where(n == nblocks - 1, 3, 2))
        for kh in range(N_KV_A):
            g0 = slice(2 * kh * WA_KV, (2 * kh + 1) * WA_KV)
            g1 = slice((2 * kh + 1) * WA_KV, (2 * kh + 2) * WA_KV)
            q2 = jnp.concatenate([q_ref[0, rows, g0], q_ref[0, rows, g1]], axis=0)
            kcat = jnp.concatenate([kbuf_ref[win, g0], kbuf_ref[win, g1]], axis=0)
            vcat = jnp.concatenate([vbuf_ref[win, g0], vbuf_ref[win, g1]], axis=0)
            s = lax.dot_general(q2, kcat, (((1,), (1,)), ((), ())), preferred_element_type=F32)
            s = s + bias_ref[variant, kh]
            probs = []
            denoms = []
            for half in range(2):
                sh = s[:, half * kv_w:(half + 1) * kv_w]
                sk = jnp.where(upper_rows, sink_ref[4 * kh + 2 + half], sink_ref[4 * kh + half])
                m = jnp.maximum(jnp.max(sh, axis=-1, keepdims=True), sk)
                p = jnp.exp2(sh - m)
                denoms.append(jnp.sum(p, axis=-1, keepdims=True) + jnp.exp2(sk - m))
                probs.append(p.astype(BF16))
            o = jnp.dot(jnp.concatenate(probs, axis=1), vcat, preferred_element_type=F32)
            o = o / jnp.where(lower_lanes, denoms[0], denoms[1])
            mix_ref[rows, g0] = o[:BLOCK].astype(BF16)
            mix_ref[rows, g1] = o[BLOCK:].astype(BF16)

        return carry

    def ret_block(i, carry):
        r0 = pl.multiple_of(i * BLOCK, BLOCK)
        rows = pl.ds(r0, BLOCK)
        for h in range(N_HEADS_R):
            sl = slice(h * HEAD_DIM_R, (h + 1) * HEAD_DIM_R)
            qh = rq_ref[0, rows, sl]
            vh = rv_ref[0, rows, sl]
            qk = lax.dot_general(qh, rk_ref[0, rows, sl], (((1,), (1,)), ((), ())),
                                 preferred_element_type=F32)
            qf = qh.astype(F32)
            acc = None
            for d, (st_ref, g_ref) in enumerate(((sf_ref, gf_ref), (sb_ref, gb_ref))):
                lhs = jnp.concatenate([(qk * dec_ref[d, h]).astype(BF16), (qf * xi_ref[d, h]).astype(BF16)], axis=1)
                rhs = jnp.concatenate([vh, st_ref[0, i, h]], axis=0)
                y = jnp.dot(lhs, rhs, preferred_element_type=F32)
                mu = jnp.mean(y, axis=-1, keepdims=True)
                yc = y - mu
                var = jnp.mean(yc * yc, axis=-1, keepdims=True)
                gate = g_ref[0, rows, sl].astype(F32)
                term = gate * jax.nn.sigmoid(gate) * (yc * lax.rsqrt(var + EPS))
                acc = term if acc is None else acc + term
            mix_ref[rows, WA_Q + h * HEAD_DIM_R:WA_Q + (h + 1) * HEAD_DIM_R] = acc.astype(BF16)
        return carry

    for i in range(nsub):
        block(i, 0)
    for i in range(nsub):
        ret_block(i, 0)

    h1 = (_residual_tile(h_ref, head_ref, c, tm, from_x)
          + jnp.dot(mix_ref[...], w_ref[...], preferred_element_type=F32))
    h1_ref[0] = h1
    ms = jnp.mean(h1 * h1, axis=-1, keepdims=True)
    xn = h1 * lax.rsqrt(ms + EPS) * g2_ref[...]
    hn_ref[0] = xn.astype(BF16)
    hi = xn.astype(BF16)
    lo = (xn - hi.astype(F32)).astype(BF16)
    both = jnp.dot(hi, wr_ref[...], preferred_element_type=F32)
    logits = (both[:, :LANES] + both[:, LANES:]
              + jnp.dot(lo, wr_ref[:, :LANES], preferred_element_type=F32))
    lt = logits.T[:N_EXPERTS]
    mx = jnp.max(lt, axis=0, keepdims=True)
    ex = jnp.exp(lt - mx)
    aff = ex / jnp.sum(ex, axis=0, keepdims=True)
    col = c * tm + lax.broadcasted_iota(I32, (1, tm), 1)
    aff_ref[0] = jnp.where(col >= PAD_FRONT, aff, -1.0)


def _mixer(lg2, sink, q, k4, v4, rq, rk, rv, gf, gb, sf, sb, bias2, h, head, from_x, w_out_bf, g2, wr_split):
    B, Lp, _ = q.shape
    D = h.shape[2]
    nb = Lp // BLOCK
    assert nb >= 3
    tm = _row_tile(Lp)
    nsub = tm // BLOCK
    kvw = k4.shape[-1]
    const = lambda b, c: (0, 0)
    cur = lambda n_: pl.BlockSpec((1, tm, n_), lambda b, c: (b, c, 0))
    prev = pl.BlockSpec((1, BLOCK, kvw), lambda b, c: (b, jnp.maximum(c * nsub - 1, 0), 0))
    nxt = pl.BlockSpec((1, BLOCK, kvw), lambda b, c: (b, jnp.minimum((c + 1) * nsub, nb - 1), 0))
    st = pl.BlockSpec((1, nsub, N_HEADS_R, HEAD_DIM_R, HEAD_DIM_R), lambda b, c: (b, c, 0, 0, 0))
    smem = pl.BlockSpec(memory_space=pltpu.SMEM)
    tab = pltpu.VMEM((2, N_HEADS_R, BLOCK, HEAD_DIM_R), F32)
    kvbuf = pltpu.VMEM((tm + 2 * BLOCK, kvw), BF16)
    return pl.pallas_call(
        functools.partial(_mixer_kernel, nblocks=nb, tm=tm, from_x=from_x),
        grid=(B, Lp // tm),
        in_specs=[smem, smem, cur(WA_Q), prev, cur(kvw), nxt, prev, cur(kvw), nxt,
                  cur(WR), cur(WR), cur(WR), cur(WR), cur(WR), st, st,
                  pl.BlockSpec(bias2.shape, lambda b, c: (0, 0, 0, 0)),
                  _residual_spec(tm, D, from_x), pl.BlockSpec((HEAD_ROWS, D), const),
                  pl.BlockSpec((D, D), const), pl.BlockSpec((1, D), const),
                  pl.BlockSpec((D, 2 * LANES), const)],
        out_specs=[cur(D), cur(D), pl.BlockSpec((1, N_EXPERTS, tm), lambda b, c: (b, 0, c))],
        out_shape=[jax.ShapeDtypeStruct((B, Lp, D), F32),
                   jax.ShapeDtypeStruct((B, Lp, D), BF16),
                   jax.ShapeDtypeStruct((B, N_EXPERTS, Lp), F32)],
        scratch_shapes=[tab, tab, kvbuf, kvbuf, pltpu.VMEM((tm, D), BF16)],
        compiler_params=_params(("arbitrary", "arbitrary"), 56 << 20),
        name="mixer",
    )(lg2, sink, q, k4, k4, k4, v4, v4, v4, rq, rk, rv, gf, gb, sf, sb, bias2, h, head, w_out_bf, g2, wr_split)


def _route_kernel(aff_ref, posm_ref, cpos_ref, *, cap, nb):
    E = aff_ref.shape[0]

    def keys(sl=slice(None)):
        return pltpu.bitcast(aff_ref[:, sl], I32)

    def count_ge(t):
        return jnp.sum((keys() >= t).astype(I32), axis=-1, keepdims=True)

    def search(_, carry):
        lo, hi = carry
        mid = lo + (hi - lo) // 2
        ok = count_ge(mid) >= cap
        return jnp.where(ok, mid, lo), jnp.where(ok, hi, mid)

    lo0 = jnp.zeros((E, 1), I32)
    hi0 = jnp.full((E, 1), 0x3F800001, I32)
    thr, _ = lax.fori_loop(0, 31, search, (lo0, hi0))
    n_gt = jnp.sum((keys() > thr).astype(F32), axis=-1, keepdims=True)
    need_eq = float(cap) - n_gt

    jr = lax.broadcasted_iota(I32, (LANES, LANES), 0)
    jc = lax.broadcasted_iota(I32, (LANES, LANES), 1)
    upper = (jr < jc).astype(F32).astype(BF16)
    lane = lax.broadcasted_iota(I32, (E, LANES), 1)

    def chunk(c, carry):
        cg, ce, cp = carry
        off = pl.multiple_of(c * LANES, LANES)
        kk = pltpu.bitcast(aff_ref[:, pl.ds(off, LANES)], I32)
        gt = kk > thr
        eq = kk == thr
        both = jnp.concatenate([gt, eq], axis=0).astype(F32)
        ex = jnp.dot(both.astype(BF16), upper, preferred_element_type=F32)
        exg = ex[:E] + cg
        exe = ex[E:] + ce
        sel = gt | (eq & (exe < need_eq))
        pos = exg + jnp.minimum(exe, need_eq)
        posm_ref[:, pl.ds(off, LANES)] = jnp.where(sel, pos, -1.0).astype(I32)
        cp = jnp.where(lane == c, (cg + jnp.minimum(ce, need_eq)).astype(I32), cp)
        tot = jnp.sum(both, axis=-1, keepdims=True)
        return cg + tot[:E], ce + tot[E:], cp

    zero = jnp.zeros((E, 1), F32)
    cg, ce, cp = lax.fori_loop(0, nb, chunk, (zero, zero, jnp.zeros((E, LANES), I32)))
    cpos_ref[...] = jnp.where(lane == nb, (cg + jnp.minimum(ce, need_eq)).astype(I32), cp)


def _route(aff_t, cap):
    B, E, Lp = aff_t.shape
    nb = Lp // LANES
    assert nb + 1 <= LANES
    posm, cpos = pl.pallas_call(
        functools.partial(_route_kernel, cap=cap, nb=nb),
        grid=(1,),
        in_specs=[pl.BlockSpec((B * E, Lp), lambda i: (0, 0))],
        out_specs=[pl.BlockSpec((B * E, Lp), lambda i: (0, 0)),
                   pl.BlockSpec((B * E, LANES), lambda i: (0, 0))],
        out_shape=[jax.ShapeDtypeStruct((B * E, Lp), I32),
                   jax.ShapeDtypeStruct((B * E, LANES), I32)],
        compiler_params=_params(("arbitrary",), 32 << 20),
        name="route",
    )(aff_t.reshape(B * E, Lp))
    return posm.reshape(B, E, Lp), cpos


NARROW_WINDOW = {1: 3 * BF16_SUBLANES, 2: 4 * BF16_SUBLANES}
MAX_CHUNK_BLOCKS = 2
WIDEST_WINDOW = MAX_CHUNK_BLOCKS * BLOCK + BF16_SUBLANES


def _wide_window(nblk):
    return nblk * BLOCK + BF16_SUBLANES


def _chunks(nsub):
    out = []
    i = 0
    while i < nsub:
        n = min(MAX_CHUNK_BLOCKS, nsub - i)
        out.append((i, n))
        i += n
    return out


def _window(cpos_ref, b, e, c, nblk, w, cpad):
    base = (b * N_EXPERTS + e) * LANES + c
    a0 = jnp.minimum((cpos_ref[base] // BF16_SUBLANES) * BF16_SUBLANES, cpad - w)
    return pl.multiple_of(a0, BF16_SUBLANES), cpos_ref[base + nblk] - a0 <= w


def _windows(cpos_ref, b, experts, c, nblk, w, cpad):
    starts = []
    fits = None
    for e in experts:
        a0, ok = _window(cpos_ref, b, e, c, nblk, w, cpad)
        starts.append(a0)
        fits = ok if fits is None else fits & ok
    return starts, fits


EXPERT_GROUP = 8
GATHER_TILE_LIMIT = 2048


def _gather_kernel(cpos_ref, hn_ref, posm_ref, x_ref, *, nsub, cpad):
    b = pl.program_id(0)
    g = pl.program_id(1)
    co = pl.program_id(2)
    experts = [g * EXPERT_GROUP + e for e in range(EXPERT_GROUP)]

    @pl.when(co == 0)
    def _():
        x_ref[...] = jnp.zeros_like(x_ref)

    def move(c, nblk, tokens, w, starts=None):
        if starts is None:
            starts = _windows(cpos_ref, b, experts, c, nblk, w, cpad)[0]
        rows = lax.broadcasted_iota(I32, (w, nblk * BLOCK), 0)
        onehots = []
        for e in range(EXPERT_GROUP):
            hit = rows + starts[e] == posm_ref[0, 0, e:e + 1, tokens]
            onehots.append(hit.astype(F32).astype(BF16))
        res = jnp.dot(jnp.concatenate(onehots, axis=0), hn_ref[0, tokens, :],
                      preferred_element_type=F32).astype(BF16)
        for e in range(EXPERT_GROUP):
            x_ref[0, e, pl.ds(starts[e], w), :] += res[e * w:(e + 1) * w]

    for first, nblk in _chunks(nsub):
        c = co * nsub + first
        tokens = slice(first * BLOCK, (first + nblk) * BLOCK)
        narrow_w = NARROW_WINDOW[nblk]
        narrow_starts, narrow = _windows(cpos_ref, b, experts, c, nblk, narrow_w, cpad)
        pl.when(narrow)(functools.partial(move, c, nblk, tokens, narrow_w, narrow_starts))
        pl.when(jnp.logical_not(narrow))(functools.partial(move, c, nblk, tokens, _wide_window(nblk)))


def _gather(cpos_flat, hn, posm, cpad):
    B, Lp, D = hn.shape
    E = N_EXPERTS
    ng = E // EXPERT_GROUP
    tm = _row_tile(Lp, GATHER_TILE_LIMIT)
    posm4 = posm.reshape(B, ng, EXPERT_GROUP, Lp)
    grid_spec = pltpu.PrefetchScalarGridSpec(
        num_scalar_prefetch=1,
        grid=(B, ng, Lp // tm),
        in_specs=[pl.BlockSpec((1, tm, D), lambda b, g, c, cp: (b, c, 0)),
                  pl.BlockSpec((1, 1, EXPERT_GROUP, tm), lambda b, g, c, cp: (b, g, 0, c))],
        out_specs=pl.BlockSpec((1, EXPERT_GROUP, cpad, D), lambda b, g, c, cp: (b, g, 0, 0)),
    )
    return pl.pallas_call(
        functools.partial(_gather_kernel, nsub=tm // BLOCK, cpad=cpad),
        grid_spec=grid_spec,
        out_shape=jax.ShapeDtypeStruct((B, E, cpad, D), BF16),
        compiler_params=_params(("arbitrary", "arbitrary", "arbitrary"), 48 << 20),
        name="gather",
    )(cpos_flat, hn, posm4)


FFN_TILE = 256
FFN_BATCH = 2


def _ffn_kernel(x_ref, wg_ref, wu_ref, wd_ref, y_ref, hm_ref, wd_bf_ref, *, tf):
    f = pl.program_id(2)
    bh, _, cpad, _ = x_ref.shape
    wg = wg_ref[0, 0].astype(BF16)
    wu = wu_ref[0, 0].astype(BF16)
    off = pl.multiple_of(f * tf, tf)
    for i in range(bh):
        x = x_ref[i, 0]
        a = jnp.dot(x, wg, preferred_element_type=F32)
        u = jnp.dot(x, wu, preferred_element_type=F32)
        hm_ref[i * cpad:(i + 1) * cpad, pl.ds(off, tf)] = (a * jax.nn.sigmoid(a) * u).astype(BF16)
    wd_bf_ref[pl.ds(off, tf), :] = wd_ref[0, 0].astype(BF16)

    @pl.when(f == pl.num_programs(2) - 1)
    def _():
        for i in range(bh):
            y_ref[i, 0] = jnp.dot(hm_ref[i * cpad:(i + 1) * cpad, :], wd_bf_ref[...],
                                  preferred_element_type=F32).astype(BF16)


def _ffn(xe, w_gate, w_up, w_down, layer):
    B, E, cpad, D = xe.shape
    F = w_gate.shape[-1]
    tf = FFN_TILE if F % FFN_TILE == 0 else LANES
    bh = FFN_BATCH if B % FFN_BATCH == 0 else 1
    xspec = pl.BlockSpec((bh, 1, cpad, D), lambda e, b, f: (b, e, 0, 0))
    return pl.pallas_call(
        functools.partial(_ffn_kernel, tf=tf),
        grid=(E, B // bh, F // tf),
        in_specs=[xspec,
                  pl.BlockSpec((1, 1, D, tf), lambda e, b, f: (layer, e, 0, f)),
                  pl.BlockSpec((1, 1, D, tf), lambda e, b, f: (layer, e, 0, f)),
                  pl.BlockSpec((1, 1, tf, D), lambda e, b, f: (layer, e, f, 0))],
        out_specs=xspec,
        out_shape=jax.ShapeDtypeStruct(xe.shape, BF16),
        scratch_shapes=[pltpu.VMEM((bh * cpad, F), BF16), pltpu.VMEM((F, D), BF16)],
        compiler_params=_params(("parallel", "parallel", "arbitrary"), 56 << 20),
        name="ffn",
    )(xe, w_gate, w_up, w_down)


def _combine_kernel(cpos_ref, h_ref, posm_ref, gate_ref, ye_ref, o_ref, *, nsub, chunk0, cpad):
    b = pl.program_id(0)
    co = pl.program_id(1)
    experts = list(range(N_EXPERTS))

    def move(c, nblk, tokens, w, starts=None):
        if starts is None:
            starts = _windows(cpos_ref, b, experts, c, nblk, w, cpad)[0]
        rows = lax.broadcasted_iota(I32, (w, nblk * BLOCK), 0)
        weights = []
        windows = []
        for e in experts:
            hit = rows + starts[e] == posm_ref[0, e:e + 1, tokens]
            weights.append(jnp.where(hit, gate_ref[0, e:e + 1, tokens], 0.0).astype(BF16))
            windows.append(ye_ref[0, e, pl.ds(starts[e], w), :])
        moe = lax.dot_general(jnp.concatenate(weights, axis=0), jnp.concatenate(windows, axis=0),
                              (((0,), (0,)), ((), ())), preferred_element_type=F32)
        o_ref[0, tokens, :] = h_ref[0, tokens, :] + moe

    for first, nblk in _chunks(nsub):
        c = chunk0 + co * nsub + first
        tokens = slice(first * BLOCK, (first + nblk) * BLOCK)
        narrow_w = NARROW_WINDOW[nblk]
        narrow_starts, narrow = _windows(cpos_ref, b, experts, c, nblk, narrow_w, cpad)
        pl.when(narrow)(functools.partial(move, c, nblk, tokens, narrow_w, narrow_starts))
        pl.when(jnp.logical_not(narrow))(functools.partial(move, c, nblk, tokens, _wide_window(nblk)))


def _combine(cpos_flat, h1, posm, gate, ye, cpad, skip_rows=0):
    B, Lp, D = h1.shape
    E = N_EXPERTS
    rows_out = Lp - skip_rows
    tm = _row_tile(rows_out)
    assert skip_rows % BLOCK == 0
    el = lambda *dims: tuple(pl.Element(d) for d in dims)
    first = lambda c: pl.multiple_of(skip_rows + c * tm, BLOCK)
    grid_spec = pltpu.PrefetchScalarGridSpec(
        num_scalar_prefetch=1,
        grid=(B, rows_out // tm),
        in_specs=[pl.BlockSpec(el(1, tm, D), lambda b, c, cp: (b, first(c), 0)),
                  pl.BlockSpec(el(1, E, tm), lambda b, c, cp: (b, 0, first(c))),
                  pl.BlockSpec(el(1, E, tm), lambda b, c, cp: (b, 0, first(c))),
                  pl.BlockSpec((1, E, cpad, D), lambda b, c, cp: (b, 0, 0, 0),
                               pipeline_mode=pl.Buffered(1))],
        out_specs=pl.BlockSpec((1, tm, D), lambda b, c, cp: (b, c, 0)),
    )
    return pl.pallas_call(
        functools.partial(_combine_kernel, nsub=tm // BLOCK, chunk0=skip_rows // BLOCK, cpad=cpad),
        grid_spec=grid_spec,
        out_shape=jax.ShapeDtypeStruct((B, rows_out, D), F32),
        compiler_params=_params(("arbitrary", "arbitrary"), VMEM_LIMIT_CAP),
        name="combine",
    )(cpos_flat, h1, posm, gate, ye)


def _t5_bucket(rel):
    half = N_BUCKETS // 2
    max_exact = half // 2
    n = jnp.abs(rel)
    large = max_exact + (jnp.log(jnp.maximum(n, max_exact).astype(F32) / max_exact)
                         / math.log(MAX_DISTANCE / max_exact) * (half - max_exact)).astype(I32)
    large = jnp.minimum(large, half - 1)
    return jnp.where(rel > 0, half, 0) + jnp.where(n < max_exact, n, large)


def _block_diag_mean(n, group):
    i = jnp.arange(n)
    return jnp.where((i[:, None] // group) == (i[None, :] // group), 1.0 / group, 0.0).astype(BF16)


def kernel(x, meta_tokens, rel_bias, norm1_g, w_in, q_norm_g, k_norm_g, attn_sink, ret_decay,
           w_out, norm2_g, w_router, w_gate, w_up, w_down):
    B, seq, D = x.shape
    depth = w_in.shape[0]
    L = seq + N_META
    Lp = L + PAD_FRONT
    assert D == D_MODEL and Lp % BLOCK == 0
    cap = CAPACITY_FACTOR * L // N_EXPERTS
    cpad = -(-max(cap, WIDEST_WINDOW) // BF16_SUBLANES) * BF16_SUBLANES

    head = jnp.concatenate([jnp.zeros((PAD_FRONT, D), x.dtype), meta_tokens.astype(x.dtype)], axis=0)
    from_x = Lp // _row_tile(Lp) > 1
    hp = x if from_x else jnp.concatenate([jnp.broadcast_to(head[None], (B, HEAD_ROWS, D)), x], axis=1)

    pos = jnp.arange(Lp, dtype=F32)
    inv = ROPE_BASE ** (-jnp.arange(0, HEAD_DIM_R, 2, dtype=F32) / HEAD_DIM_R)
    ang = pos[:, None] * inv[None]
    cc = jnp.concatenate([jnp.cos(ang), jnp.cos(ang)], axis=-1)
    ss = jnp.concatenate([-jnp.sin(ang), jnp.sin(ang)], axis=-1)
    rel = (jnp.arange(3 * BLOCK)[None, :] - BLOCK) - jnp.arange(BLOCK)[:, None]
    bucket_onehot = (_t5_bucket(rel)[None] == jnp.arange(N_BUCKETS)[:, None, None]).astype(F32)
    bias = jnp.einsum('kqs,kh->hqs', bucket_onehot, rel_bias.astype(F32), precision=lax.Precision.HIGHEST)
    bias = jnp.where((jnp.abs(rel) <= BLOCK)[None], bias * LOG2E, NEG)
    key = jnp.arange(3 * BLOCK)
    bias = jnp.stack([jnp.where(key >= BLOCK + PAD_FRONT, bias, NEG), jnp.where(key >= PAD_FRONT, bias, NEG),
                      bias, jnp.where(key < 2 * BLOCK, bias, NEG)])
    bias = bias.reshape(4, N_KV_A, 2, 2, BLOCK, 3 * BLOCK).transpose(0, 1, 2, 4, 3, 5)
    bias = bias.reshape(4, N_KV_A, 2 * BLOCK, 6 * BLOCK)
    bdq = _block_diag_mean(WA_Q, HEAD_DIM_A)
    bdk = _block_diag_mean(WA_KV, HEAD_DIM_A)

    for l in range(depth):
        qg = (jnp.tile(q_norm_g[l].astype(F32), N_HEADS_A) * (HEAD_DIM_A ** -0.5 * LOG2E))[None]
        kg = jnp.tile(k_norm_g[l].astype(F32), N_KV_A)[None]
        lg2 = -jnp.exp(ret_decay[l].astype(F32))
        first = from_x and l == 0
        q, k, v, rq, rk, rv, gf, gb = _in_proj(hp, head, first, norm1_g[l][None].astype(F32),
                                               w_in[l].astype(BF16), cc, ss, qg, kg, bdq, bdk)
        sf, sb = _ret_states(lg2, rk, rv)
        wr = jnp.pad(w_router[l].astype(F32), ((0, 0), (0, LANES - N_EXPERTS)))
        wr_hi = wr.astype(BF16)
        wr_lo = (wr - wr_hi.astype(F32)).astype(BF16)
        wr_split = jnp.concatenate([wr_hi, wr_lo], axis=1)
        h1, hn, aff = _mixer(lg2, attn_sink[l].astype(F32) * LOG2E, q, k, v, rq, rk, rv, gf, gb, sf, sb, bias,
                             hp, head, first, w_out[l].astype(BF16), norm2_g[l][None].astype(F32), wr_split)
        posm, cpos = _route(aff, cap)
        cpos_flat = cpos.reshape(-1)
        xe = _gather(cpos_flat, hn, posm, cpad)
        ye = _ffn(xe, w_gate, w_up, w_down, l)
        last = l == depth - 1
        hp = _combine(cpos_flat, h1, posm, aff, ye, cpad, skip_rows=(PAD_FRONT + N_META) if last else 0)
    return hp
```

```python
import functools
import math

import jax
import jax.numpy as jnp
from jax import lax
from jax.experimental import pallas as pl
from jax.experimental.pallas import tpu as pltpu

D_MODEL = 1024
N_META = 16
BLOCK = 128
PAD_FRONT = BLOCK - N_META
EPS = 1e-6
N_HEADS_A = 8
N_KV_A = 2
GROUP_A = N_HEADS_A // N_KV_A
HEAD_DIM_A = 64
N_HEADS_R = 4
HEAD_DIM_R = 128
ROPE_BASE = 10000.0
N_BUCKETS = 32
MAX_DISTANCE = 128
N_EXPERTS = 16
CAPACITY_FACTOR = 2
WA_Q = N_HEADS_A * HEAD_DIM_A
WA_KV = N_KV_A * HEAD_DIM_A
WR = N_HEADS_R * HEAD_DIM_R
D_IN = WA_Q + 2 * WA_KV + 5 * WR
NEG = -1e30
LOG2E = math.log2(math.e)

LANES = 128
BF16_SUBLANES = 16
VMEM_LIMIT_CAP = 60000 * 1024

F32 = jnp.float32
BF16 = jnp.bfloat16
I32 = jnp.int32


def _params(semantics, vmem_bytes, **extra):
    return pltpu.CompilerParams(dimension_semantics=semantics,
                                vmem_limit_bytes=min(int(vmem_bytes), VMEM_LIMIT_CAP), **extra)


def _row_tile(lp, limit=1024):
    best = BLOCK
    for t in range(BLOCK, limit + 1, BLOCK):
        if lp % t == 0:
            best = t
    return best


HEAD_ROWS = PAD_FRONT + N_META


def _residual_tile(h_ref, head_ref, c, tm, from_x):
    blk = h_ref[0]
    if not from_x:
        return blk
    first = jnp.concatenate([head_ref[...], blk[:tm - HEAD_ROWS]], axis=0)
    return jnp.where(c == 0, first, blk)


def _residual_spec(tm, d, from_x):
    if not from_x:
        return pl.BlockSpec((1, tm, d), lambda b, c: (b, c, 0))
    start = lambda c: pl.multiple_of(jnp.maximum(c * tm - HEAD_ROWS, 0), BLOCK)
    return pl.BlockSpec((pl.Element(1), pl.Element(tm), pl.Element(d)), lambda b, c: (b, start(c), 0))


def _in_proj_kernel(h_ref, head_ref, g1_ref, w_ref, cc_ref, ss_ref, qg_ref, kg_ref, bdq_ref, bdk_ref,
                    q_ref, k_ref, v_ref, rq_ref, rk_ref, rv_ref, gf_ref, gb_ref, *, tm, from_x):
    j = pl.program_id(1)
    x = _residual_tile(h_ref, head_ref, j, tm, from_x)
    ms = jnp.mean(x * x, axis=-1, keepdims=True)
    xn = x * lax.rsqrt(ms + EPS) * g1_ref[...]
    row = j * tm + lax.broadcasted_iota(I32, (tm, 1), 0)
    xn = jnp.where(row >= PAD_FRONT, xn, 0.0).astype(BF16)

    def proj(lo, n):
        return jnp.dot(xn, w_ref[:, lo:lo + n], preferred_element_type=F32)

    q = proj(0, WA_Q)
    q_ms = jnp.dot((q * q).astype(BF16), bdq_ref[...], preferred_element_type=F32)
    q_ref[0] = (q * lax.rsqrt(q_ms + EPS) * qg_ref[...]).astype(BF16)
    k = proj(WA_Q, WA_KV)
    k_ms = jnp.dot((k * k).astype(BF16), bdk_ref[...], preferred_element_type=F32)
    k = k * lax.rsqrt(k_ms + EPS) * kg_ref[...]
    v = proj(WA_Q + WA_KV, WA_KV)
    lower = lax.broadcasted_iota(I32, (1, WA_KV), 1) < HEAD_DIM_A
    for t, ref in ((k, k_ref), (v, v_ref)):
        head0_lo = jnp.where(lower, t, 0.0)
        head1_hi = jnp.where(lower, 0.0, t)
        variants = (head0_lo, pltpu.roll(head0_lo, HEAD_DIM_A, 1), pltpu.roll(head1_hi, HEAD_DIM_A, 1), head1_hi)
        for i, var in enumerate(variants):
            ref[0, :, i * WA_KV:(i + 1) * WA_KV] = var.astype(BF16)

    cc = cc_ref[...]
    ss = ss_ref[...]
    base = WA_Q + 2 * WA_KV
    for off, ref, scale in ((base, rq_ref, 1.0), (base + WR, rk_ref, HEAD_DIM_R ** -0.5)):
        r = proj(off, WR)
        for hh in range(N_HEADS_R):
            seg = r[:, hh * HEAD_DIM_R:(hh + 1) * HEAD_DIM_R]
            rot = seg * cc + pltpu.roll(seg, HEAD_DIM_R // 2, 1) * ss
            ref[0, :, hh * HEAD_DIM_R:(hh + 1) * HEAD_DIM_R] = (rot * scale).astype(BF16)
    rv_ref[0] = proj(base + 2 * WR, WR).astype(BF16)
    gf_ref[0] = proj(base + 3 * WR, WR).astype(BF16)
    gb_ref[0] = proj(base + 4 * WR, WR).astype(BF16)


def _in_proj(h, head, from_x, g1, w_in_bf, cc, ss, qg, kg, bdq, bdk):
    B, D = h.shape[0], h.shape[2]
    Lp = cc.shape[0]
    tm = _row_tile(Lp)
    const = lambda b, j: (0, 0)
    row = lambda n: pl.BlockSpec((1, tm, n), lambda b, j: (b, j, 0))
    widths = (WA_Q, 2 * N_KV_A * WA_KV, 2 * N_KV_A * WA_KV, WR, WR, WR, WR, WR)
    return pl.pallas_call(
        functools.partial(_in_proj_kernel, tm=tm, from_x=from_x),
        grid=(B, Lp // tm),
        in_specs=[_residual_spec(tm, D, from_x),
                  pl.BlockSpec((HEAD_ROWS, D), const),
                  pl.BlockSpec((1, D), const),
                  pl.BlockSpec((D, D_IN), const),
                  pl.BlockSpec((tm, HEAD_DIM_R), lambda b, j: (j, 0)),
                  pl.BlockSpec((tm, HEAD_DIM_R), lambda b, j: (j, 0)),
                  pl.BlockSpec((1, WA_Q), const),
                  pl.BlockSpec((1, WA_KV), const),
                  pl.BlockSpec((WA_Q, WA_Q), const),
                  pl.BlockSpec((WA_KV, WA_KV), const)],
        out_specs=[row(n) for n in widths],
        out_shape=[jax.ShapeDtypeStruct((B, Lp, n), BF16) for n in widths],
        compiler_params=_params(("parallel", "parallel"), 48 << 20),
        name="in_proj",
    )(h, head, g1, w_in_bf, cc, ss, qg, kg, bdq, bdk)


def _ret_state_kernel(lg_ref, kf_ref, vf_ref, kb_ref, vb_ref, sf_ref, sb_ref, st_ref, *, nsub):
    t = pl.program_id(0)

    @pl.when(t == 0)
    def _():
        st_ref[...] = jnp.zeros_like(st_ref)

    idx = lax.broadcasted_iota(I32, (BLOCK, 1), 0).astype(F32)
    ones_row = jnp.ones((1, HEAD_DIM_R), F32)
    for d, (k_ref, v_ref, s_out) in enumerate(((kf_ref, vf_ref, sf_ref), (kb_ref, vb_ref, sb_ref))):
        for h in range(N_HEADS_R):
            lg = lg_ref[d, h]
            sl = slice(h * HEAD_DIM_R, (h + 1) * HEAD_DIM_R)
            zeta = jnp.exp(lg * ((BLOCK - 1.0 - idx) if d == 0 else idx))
            chunk_decay = jnp.exp(lg * float(BLOCK) * ones_row)
            for b in range(st_ref.shape[0]):
                st = st_ref[b, d, h]
                for step in range(nsub):
                    j = step if d == 0 else nsub - 1 - step
                    rows = slice(j * BLOCK, (j + 1) * BLOCK)
                    s_out[b, j, h] = st.astype(BF16)
                    kz = (k_ref[b, rows, sl].astype(F32) * zeta).astype(BF16)
                    kv = lax.dot_general(kz, v_ref[b, rows, sl], (((0,), (0,)), ((), ())),
                                         preferred_element_type=F32)
                    st = st * chunk_decay + kv
                st_ref[b, d, h] = st


def _ret_states(lg2, rk, rv):
    B, Lp, _ = rk.shape
    tm = _row_tile(Lp)
    nsub = tm // BLOCK
    nt = Lp // tm
    fwd = pl.BlockSpec((B, tm, WR), lambda t: (0, t, 0))
    bwd = pl.BlockSpec((B, tm, WR), lambda t: (0, nt - 1 - t, 0))
    st_shape = (B, nsub, N_HEADS_R, HEAD_DIM_R, HEAD_DIM_R)
    return pl.pallas_call(
        functools.partial(_ret_state_kernel, nsub=nsub),
        grid=(nt,),
        in_specs=[pl.BlockSpec(memory_space=pltpu.SMEM), fwd, fwd, bwd, bwd],
        out_specs=[pl.BlockSpec(st_shape, lambda t: (0, t, 0, 0, 0)),
                   pl.BlockSpec(st_shape, lambda t: (0, nt - 1 - t, 0, 0, 0))],
        out_shape=[jax.ShapeDtypeStruct((B, Lp // BLOCK) + st_shape[2:], BF16)] * 2,
        scratch_shapes=[pltpu.VMEM((B, 2, N_HEADS_R, HEAD_DIM_R, HEAD_DIM_R), F32)],
        compiler_params=_params(("arbitrary",), 48 << 20),
        name="ret_state",
    )(lg2, rk, rv, rk, rv)


def _mixer_kernel(lg_ref, sink_ref, q_ref, kp_ref, kc_ref, kn_ref, vp_ref, vc_ref, vn_ref,
                  rq_ref, rk_ref, rv_ref, gf_ref, gb_ref, sf_ref, sb_ref, bias_ref,
                  h_ref, head_ref, w_ref, g2_ref, wr_ref, h1_ref, hn_ref, aff_ref,
                  dec_ref, xi_ref, kbuf_ref, vbuf_ref, mix_ref, *, nblocks, tm, from_x):
    b = pl.program_id(0)
    c = pl.program_id(1)
    nsub = tm // BLOCK
    kv_w = 3 * BLOCK

    @pl.when((b == 0) & (c == 0))
    def _():
        ii = lax.broadcasted_iota(I32, (BLOCK, BLOCK), 0).astype(F32)
        jj = lax.broadcasted_iota(I32, (BLOCK, BLOCK), 1).astype(F32)
        for d in range(2):
            diff = (ii - jj) if d == 0 else (jj - ii)
            reach = (ii + 1.0) if d == 0 else (float(BLOCK) - ii)
            for h in range(N_HEADS_R):
                lg = lg_ref[d, h]
                dec_ref[d, h] = jnp.where(diff >= 0, jnp.exp(lg * jnp.maximum(diff, 0.0)), 0.0)
                xi_ref[d, h] = jnp.exp(lg * reach)

    for buf, (p_ref, c_ref, n_ref) in ((kbuf_ref, (kp_ref, kc_ref, kn_ref)), (vbuf_ref, (vp_ref, vc_ref, vn_ref))):
        buf[0:BLOCK] = p_ref[0]
        buf[BLOCK:BLOCK + tm] = c_ref[0]
        buf[BLOCK + tm:] = n_ref[0]

    lower_lanes = lax.broadcasted_iota(I32, (1, 2 * HEAD_DIM_A), 1) < HEAD_DIM_A
    upper_rows = lax.broadcasted_iota(I32, (2 * BLOCK, 1), 0) >= BLOCK

    def block(i, carry):
        r0 = pl.multiple_of(i * BLOCK, BLOCK)
        rows = pl.ds(r0, BLOCK)
        win = pl.ds(r0, kv_w)

        n = c * nsub + i
        variant = jnp.where(n < 2, n, jnp.where(n == nblocks - 1, 3, 2))
        for kh in range(N_KV_A):
            g0 = slice(2 * kh * WA_KV, (2 * kh + 1) * WA_KV)
            g1 = slice((2 * kh + 1) * WA_KV, (2 * kh + 2) * WA_KV)
            q2 = jnp.concatenate([q_ref[0, rows, g0], q_ref[0, rows, g1]], axis=0)
            kcat = jnp.concatenate([kbuf_ref[win, g0], kbuf_ref[win, g1]], axis=0)
            vcat = jnp.concatenate([vbuf_ref[win, g0], vbuf_ref[win, g1]], axis=0)
            s = lax.dot_general(q2, kcat, (((1,), (1,)), ((), ())), preferred_element_type=F32)
            s = s + bias_ref[variant, kh]
            probs = []
            denoms = []
            for half in range(2):
                sh = s[:, half * kv_w:(half + 1) * kv_w]
                sk = jnp.where(upper_rows, sink_ref[4 * kh + 2 + half], sink_ref[4 * kh + half])
                m = jnp.maximum(jnp.max(sh, axis=-1, keepdims=True), sk)
                p = jnp.exp2(sh - m)
                denoms.append(jnp.sum(p, axis=-1, keepdims=True) + jnp.exp2(sk - m))
                probs.append(p.astype(BF16))
            o = jnp.dot(jnp.concatenate(probs, axis=1), vcat, preferred_element_type=F32)
            o = o / jnp.where(lower_lanes, denoms[0], denoms[1])
            mix_ref[rows, g0] = o[:BLOCK].astype(BF16)
            mix_ref[rows, g1] = o[BLOCK:].astype(BF16)

        return carry

    def ret_block(i, carry):
        r0 = pl.multiple_of(i * BLOCK, BLOCK)
        rows = pl.ds(r0, BLOCK)
        for h in range(N_HEADS_R):
            sl = slice(h * HEAD_DIM_R, (h + 1) * HEAD_DIM_R)
            qh = rq_ref[0, rows, sl]
            vh = rv_ref[0, rows, sl]
            qk = lax.dot_general(qh, rk_ref[0, rows, sl], (((1,), (1,)), ((), ())),
                                 preferred_element_type=F32)
            qf = qh.astype(F32)
            acc = None
            for d, (st_ref, g_ref) in enumerate(((sf_ref, gf_ref), (sb_ref, gb_ref))):
                lhs = jnp.concatenate([(qk * dec_ref[d, h]).astype(BF16), (qf * xi_ref[d, h]).astype(BF16)], axis=1)
                rhs = jnp.concatenate([vh, st_ref[0, i, h]], axis=0)
                y = jnp.dot(lhs, rhs, preferred_element_type=F32)
                mu = jnp.mean(y, axis=-1, keepdims=True)
                yc = y - mu
                var = jnp.mean(yc * yc, axis=-1, keepdims=True)
                gate = g_ref[0, rows, sl].astype(F32)
                term = gate * jax.nn.sigmoid(gate) * (yc * lax.rsqrt(var + EPS))
                acc = term if acc is None else acc + term
            mix_ref[rows, WA_Q + h * HEAD_DIM_R:WA_Q + (h + 1) * HEAD_DIM_R] = acc.astype(BF16)
        return carry

    for i in range(nsub):
        block(i, 0)
    for i in range(nsub):
        ret_block(i, 0)

    h1 = (_residual_tile(h_ref, head_ref, c, tm, from_x)
          + jnp.dot(mix_ref[...], w_ref[...], preferred_element_type=F32))
    h1_ref[0] = h1
    ms = jnp.mean(h1 * h1, axis=-1, keepdims=True)
    xn = h1 * lax.rsqrt(ms + EPS) * g2_ref[...]
    hn_ref[0] = xn.astype(BF16)
    hi = xn.astype(BF16)
    lo = (xn - hi.astype(F32)).astype(BF16)
    both = jnp.dot(hi, wr_ref[...], preferred_element_type=F32)
    logits = (both[:, :LANES] + both[:, LANES:]
              + jnp.dot(lo, wr_ref[:, :LANES], preferred_element_type=F32))
    lt = logits.T[:N_EXPERTS]
    mx = jnp.max(lt, axis=0, keepdims=True)
    ex = jnp.exp(lt - mx)
    aff = ex / jnp.sum(ex, axis=0, keepdims=True)
    col = c * tm + lax.broadcasted_iota(I32, (1, tm), 1)
    aff_ref[0] = jnp.where(col >= PAD_FRONT, aff, -1.0)


def _mixer(lg2, sink, q, k4, v4, rq, rk, rv, gf, gb, sf, sb, bias2, h, head, from_x, w_out_bf, g2, wr_split):
    B, Lp, _ = q.shape
    D = h.shape[2]
    nb = Lp // BLOCK
    assert nb >= 3
    tm = _row_tile(Lp)
    nsub = tm // BLOCK
    kvw = k4.shape[-1]
    const = lambda b, c: (0, 0)
    cur = lambda n_: pl.BlockSpec((1, tm, n_), lambda b, c: (b, c, 0))
    prev = pl.BlockSpec((1, BLOCK, kvw), lambda b, c: (b, jnp.maximum(c * nsub - 1, 0), 0))
    nxt = pl.BlockSpec((1, BLOCK, kvw), lambda b, c: (b, jnp.minimum((c + 1) * nsub, nb - 1), 0))
    st = pl.BlockSpec((1, nsub, N_HEADS_R, HEAD_DIM_R, HEAD_DIM_R), lambda b, c: (b, c, 0, 0, 0))
    smem = pl.BlockSpec(memory_space=pltpu.SMEM)
    tab = pltpu.VMEM((2, N_HEADS_R, BLOCK, HEAD_DIM_R), F32)
    kvbuf = pltpu.VMEM((tm + 2 * BLOCK, kvw), BF16)
    return pl.pallas_call(
        functools.partial(_mixer_kernel, nblocks=nb, tm=tm, from_x=from_x),
        grid=(B, Lp // tm),
        in_specs=[smem, smem, cur(WA_Q), prev, cur(kvw), nxt, prev, cur(kvw), nxt,
                  cur(WR), cur(WR), cur(WR), cur(WR), cur(WR), st, st,
                  pl.BlockSpec(bias2.shape, lambda b, c: (0, 0, 0, 0)),
                  _residual_spec(tm, D, from_x), pl.BlockSpec((HEAD_ROWS, D), const),
                  pl.BlockSpec((D, D), const), pl.BlockSpec((1, D), const),
                  pl.BlockSpec((D, 2 * LANES), const)],
        out_specs=[cur(D), cur(D), pl.BlockSpec((1, N_EXPERTS, tm), lambda b, c: (b, 0, c))],
        out_shape=[jax.ShapeDtypeStruct((B, Lp, D), F32),
                   jax.ShapeDtypeStruct((B, Lp, D), BF16),
                   jax.ShapeDtypeStruct((B, N_EXPERTS, Lp), F32)],
        scratch_shapes=[tab, tab, kvbuf, kvbuf, pltpu.VMEM((tm, D), BF16)],
        compiler_params=_params(("arbitrary", "arbitrary"), 56 << 20),
        name="mixer",
    )(lg2, sink, q, k4, k4, k4, v4, v4, v4, rq, rk, rv, gf, gb, sf, sb, bias2, h, head, w_out_bf, g2, wr_split)


def _route_kernel(aff_ref, posm_ref, cpos_ref, *, cap, nb):
    E = aff_ref.shape[0]

    def keys(sl=slice(None)):
        return pltpu.bitcast(aff_ref[:, sl], I32)

    def count_ge(t):
        return jnp.sum((keys() >= t).astype(I32), axis=-1, keepdims=True)

    def search(_, carry):
        lo, hi = carry
        mid = lo + (hi - lo) // 2
        ok = count_ge(mid) >= cap
        return jnp.where(ok, mid, lo), jnp.where(ok, hi, mid)

    lo0 = jnp.zeros((E, 1), I32)
    hi0 = jnp.full((E, 1), 0x3F800001, I32)
    thr, _ = lax.fori_loop(0, 31, search, (lo0, hi0))
    n_gt = jnp.sum((keys() > thr).astype(F32), axis=-1, keepdims=True)
    need_eq = float(cap) - n_gt

    jr = lax.broadcasted_iota(I32, (LANES, LANES), 0)
    jc = lax.broadcasted_iota(I32, (LANES, LANES), 1)
    upper = (jr < jc).astype(F32).astype(BF16)
    lane = lax.broadcasted_iota(I32, (E, LANES), 1)

    def chunk(c, carry):
        cg, ce, cp = carry
        off = pl.multiple_of(c * LANES, LANES)
        kk = pltpu.bitcast(aff_ref[:, pl.ds(off, LANES)], I32)
        gt = kk > thr
        eq = kk == thr
        both = jnp.concatenate([gt, eq], axis=0).astype(F32)
        ex = jnp.dot(both.astype(BF16), upper, preferred_element_type=F32)
        exg = ex[:E] + cg
        exe = ex[E:] + ce
        sel = gt | (eq & (exe < need_eq))
        pos = exg + jnp.minimum(exe, need_eq)
        posm_ref[:, pl.ds(off, LANES)] = jnp.where(sel, pos, -1.0).astype(I32)
        cp = jnp.where(lane == c, (cg + jnp.minimum(ce, need_eq)).astype(I32), cp)
        tot = jnp.sum(both, axis=-1, keepdims=True)
        return cg + tot[:E], ce + tot[E:], cp

    zero = jnp.zeros((E, 1), F32)
    cg, ce, cp = lax.fori_loop(0, nb, chunk, (zero, zero, jnp.zeros((E, LANES), I32)))
    cpos_ref[...] = jnp.where(lane == nb, (cg + jnp.minimum(ce, need_eq)).astype(I32), cp)


def _route(aff_t, cap):
    B, E, Lp = aff_t.shape
    nb = Lp // LANES
    assert nb + 1 <= LANES
    posm, cpos = pl.pallas_call(
        functools.partial(_route_kernel, cap=cap, nb=nb),
        grid=(1,),
        in_specs=[pl.BlockSpec((B * E, Lp), lambda i: (0, 0))],
        out_specs=[pl.BlockSpec((B * E, Lp), lambda i: (0, 0)),
                   pl.BlockSpec((B * E, LANES), lambda i: (0, 0))],
        out_shape=[jax.ShapeDtypeStruct((B * E, Lp), I32),
                   jax.ShapeDtypeStruct((B * E, LANES), I32)],
        compiler_params=_params(("arbitrary",), 32 << 20),
        name="route",
    )(aff_t.reshape(B * E, Lp))
    return posm.reshape(B, E, Lp), cpos


NARROW_WINDOW = {1: 3 * BF16_SUBLANES, 2: 4 * BF16_SUBLANES}
MAX_CHUNK_BLOCKS = 2
WIDEST_WINDOW = MAX_CHUNK_BLOCKS * BLOCK + BF16_SUBLANES


def _wide_window(nblk):
    return nblk * BLOCK + BF16_SUBLANES


def _chunks(nsub):
    out = []
    i = 0
    while i < nsub:
        n = min(MAX_CHUNK_BLOCKS, nsub - i)
        out.append((i, n))
        i += n
    return out


def _window(cpos_ref, b, e, c, nblk, w, cpad):
    base = (b * N_EXPERTS + e) * LANES + c
    a0 = jnp.minimum((cpos_ref[base] // BF16_SUBLANES) * BF16_SUBLANES, cpad - w)
    return pl.multiple_of(a0, BF16_SUBLANES), cpos_ref[base + nblk] - a0 <= w


def _windows(cpos_ref, b, experts, c, nblk, w, cpad):
    starts = []
    fits = None
    for e in experts:
        a0, ok = _window(cpos_ref, b, e, c, nblk, w, cpad)
        starts.append(a0)
        fits = ok if fits is None else fits & ok
    return starts, fits


EXPERT_GROUP = 16
GATHER_TILE_LIMIT = 2048


def _gather_kernel(cpos_ref, hn_ref, posm_ref, x_ref, *, nsub, cpad):
    b = pl.program_id(0)
    g = pl.program_id(1)
    co = pl.program_id(2)
    experts = [g * EXPERT_GROUP + e for e in range(EXPERT_GROUP)]

    @pl.when(co == 0)
    def _():
        x_ref[...] = jnp.zeros_like(x_ref)

    def move(c, nblk, tokens, w, starts=None):
        if starts is None:
            starts = _windows(cpos_ref, b, experts, c, nblk, w, cpad)[0]
        rows = lax.broadcasted_iota(I32, (w, nblk * BLOCK), 0)
        onehots = []
        for e in range(EXPERT_GROUP):
            hit = rows + starts[e] == posm_ref[0, 0, e:e + 1, tokens]
            onehots.append(hit.astype(F32).astype(BF16))
        res = jnp.dot(jnp.concatenate(onehots, axis=0), hn_ref[0, tokens, :],
                      preferred_element_type=F32).astype(BF16)
        for e in range(EXPERT_GROUP):
            x_ref[0, e, pl.ds(starts[e], w), :] += res[e * w:(e + 1) * w]

    for first, nblk in _chunks(nsub):
        c = co * nsub + first
        tokens = slice(first * BLOCK, (first + nblk) * BLOCK)
        narrow_w = NARROW_WINDOW[nblk]
        narrow_starts, narrow = _windows(cpos_ref, b, experts, c, nblk, narrow_w, cpad)
        pl.when(narrow)(functools.partial(move, c, nblk, tokens, narrow_w, narrow_starts))
        pl.when(jnp.logical_not(narrow))(functools.partial(move, c, nblk, tokens, _wide_window(nblk)))


def _gather(cpos_flat, hn, posm, cpad):
    B, Lp, D = hn.shape
    E = N_EXPERTS
    ng = E // EXPERT_GROUP
    tm = _row_tile(Lp, GATHER_TILE_LIMIT)
    posm4 = posm.reshape(B, ng, EXPERT_GROUP, Lp)
    grid_spec = pltpu.PrefetchScalarGridSpec(
        num_scalar_prefetch=1,
        grid=(B, ng, Lp // tm),
        in_specs=[pl.BlockSpec((1, tm, D), lambda b, g, c, cp: (b, c, 0)),
                  pl.BlockSpec((1, 1, EXPERT_GROUP, tm), lambda b, g, c, cp: (b, g, 0, c))],
        out_specs=pl.BlockSpec((1, EXPERT_GROUP, cpad, D), lambda b, g, c, cp: (b, g, 0, 0),
                               pipeline_mode=pl.Buffered(1)),
    )
    return pl.pallas_call(
        functools.partial(_gather_kernel, nsub=tm // BLOCK, cpad=cpad),
        grid_spec=grid_spec,
        out_shape=jax.ShapeDtypeStruct((B, E, cpad, D), BF16),
        compiler_params=_params(("arbitrary", "arbitrary", "arbitrary"), VMEM_LIMIT_CAP),
        name="gather",
    )(cpos_flat, hn, posm4)


FFN_TILE = 256
FFN_BATCH = 2


def _ffn_kernel(x_ref, wg_ref, wu_ref, wd_ref, y_ref, hm_ref, wd_bf_ref, *, tf):
    f = pl.program_id(2)
    bh, _, cpad, _ = x_ref.shape
    wg = wg_ref[0, 0].astype(BF16)
    wu = wu_ref[0, 0].astype(BF16)
    off = pl.multiple_of(f * tf, tf)
    for i in range(bh):
        x = x_ref[i, 0]
        a = jnp.dot(x, wg, preferred_element_type=F32)
        u = jnp.dot(x, wu, preferred_element_type=F32)
        hm_ref[i * cpad:(i + 1) * cpad, pl.ds(off, tf)] = (a * jax.nn.sigmoid(a) * u).astype(BF16)
    wd_bf_ref[pl.ds(off, tf), :] = wd_ref[0, 0].astype(BF16)

    @pl.when(f == pl.num_programs(2) - 1)
    def _():
        for i in range(bh):
            y_ref[i, 0] = jnp.dot(hm_ref[i * cpad:(i + 1) * cpad, :], wd_bf_ref[...],
                                  preferred_element_type=F32).astype(BF16)


def _ffn(xe, w_gate, w_up, w_down, layer):
    B, E, cpad, D = xe.shape
    F = w_gate.shape[-1]
    tf = FFN_TILE if F % FFN_TILE == 0 else LANES
    bh = FFN_BATCH if B % FFN_BATCH == 0 else 1
    xspec = pl.BlockSpec((bh, 1, cpad, D), lambda e, b, f: (b, e, 0, 0))
    return pl.pallas_call(
        functools.partial(_ffn_kernel, tf=tf),
        grid=(E, B // bh, F // tf),
        in_specs=[xspec,
                  pl.BlockSpec((1, 1, D, tf), lambda e, b, f: (layer, e, 0, f)),
                  pl.BlockSpec((1, 1, D, tf), lambda e, b, f: (layer, e, 0, f)),
                  pl.BlockSpec((1, 1, tf, D), lambda e, b, f: (layer, e, f, 0))],
        out_specs=xspec,
        out_shape=jax.ShapeDtypeStruct(xe.shape, BF16),
        scratch_shapes=[pltpu.VMEM((bh * cpad, F), BF16), pltpu.VMEM((F, D), BF16)],
        compiler_params=_params(("parallel", "parallel", "arbitrary"), 56 << 20),
        name="ffn",
    )(xe, w_gate, w_up, w_down)


def _combine_kernel(cpos_ref, h_ref, posm_ref, gate_ref, ye_ref, o_ref, *, nsub, chunk0, cpad):
    b = pl.program_id(0)
    co = pl.program_id(1)
    experts = list(range(N_EXPERTS))

    def move(c, nblk, tokens, w, starts=None):
        if starts is None:
            starts = _windows(cpos_ref, b, experts, c, nblk, w, cpad)[0]
        rows = lax.broadcasted_iota(I32, (w, nblk * BLOCK), 0)
        weights = []
        windows = []
        for e in experts:
            hit = rows + starts[e] == posm_ref[0, e:e + 1, tokens]
            weights.append(jnp.where(hit, gate_ref[0, e:e + 1, tokens], 0.0).astype(BF16))
            windows.append(ye_ref[0, e, pl.ds(starts[e], w), :])
        moe = lax.dot_general(jnp.concatenate(weights, axis=0), jnp.concatenate(windows, axis=0),
                              (((0,), (0,)), ((), ())), preferred_element_type=F32)
        o_ref[0, tokens, :] = h_ref[0, tokens, :] + moe

    for first, nblk in _chunks(nsub):
        c = chunk0 + co * nsub + first
        tokens = slice(first * BLOCK, (first + nblk) * BLOCK)
        narrow_w = NARROW_WINDOW[nblk]
        narrow_starts, narrow = _windows(cpos_ref, b, experts, c, nblk, narrow_w, cpad)
        pl.when(narrow)(functools.partial(move, c, nblk, tokens, narrow_w, narrow_starts))
        pl.when(jnp.logical_not(narrow))(functools.partial(move, c, nblk, tokens, _wide_window(nblk)))


def _combine(cpos_flat, h1, posm, gate, ye, cpad, skip_rows=0):
    B, Lp, D = h1.shape
    E = N_EXPERTS
    rows_out = Lp - skip_rows
    tm = _row_tile(rows_out)
    assert skip_rows % BLOCK == 0
    el = lambda *dims: tuple(pl.Element(d) for d in dims)
    first = lambda c: pl.multiple_of(skip_rows + c * tm, BLOCK)
    grid_spec = pltpu.PrefetchScalarGridSpec(
        num_scalar_prefetch=1,
        grid=(B, rows_out // tm),
        in_specs=[pl.BlockSpec(el(1, tm, D), lambda b, c, cp: (b, first(c), 0)),
                  pl.BlockSpec(el(1, E, tm), lambda b, c, cp: (b, 0, first(c))),
                  pl.BlockSpec(el(1, E, tm), lambda b, c, cp: (b, 0, first(c))),
                  pl.BlockSpec((1, E, cpad, D), lambda b, c, cp: (b, 0, 0, 0),
                               pipeline_mode=pl.Buffered(1))],
        out_specs=pl.BlockSpec((1, tm, D), lambda b, c, cp: (b, c, 0)),
    )
    return pl.pallas_call(
        functools.partial(_combine_kernel, nsub=tm // BLOCK, chunk0=skip_rows // BLOCK, cpad=cpad),
        grid_spec=grid_spec,
        out_shape=jax.ShapeDtypeStruct((B, rows_out, D), F32),
        compiler_params=_params(("arbitrary", "arbitrary"), VMEM_LIMIT_CAP),
        name="combine",
    )(cpos_flat, h1, posm, gate, ye)


def _t5_bucket(rel):
    half = N_BUCKETS // 2
    max_exact = half // 2
    n = jnp.abs(rel)
    large = max_exact + (jnp.log(jnp.maximum(n, max_exact).astype(F32) / max_exact)
                         / math.log(MAX_DISTANCE / max_exact) * (half - max_exact)).astype(I32)
    large = jnp.minimum(large, half - 1)
    return jnp.where(rel > 0, half, 0) + jnp.where(n < max_exact, n, large)


def _block_diag_mean(n, group):
    i = jnp.arange(n)
    return jnp.where((i[:, None] // group) == (i[None, :] // group), 1.0 / group, 0.0).astype(BF16)


def kernel(x, meta_tokens, rel_bias, norm1_g, w_in, q_norm_g, k_norm_g, attn_sink, ret_decay,
           w_out, norm2_g, w_router, w_gate, w_up, w_down):
    B, seq, D = x.shape
    depth = w_in.shape[0]
    L = seq + N_META
    Lp = L + PAD_FRONT
    assert D == D_MODEL and Lp % BLOCK == 0
    cap = CAPACITY_FACTOR * L // N_EXPERTS
    cpad = -(-max(cap, WIDEST_WINDOW) // BF16_SUBLANES) * BF16_SUBLANES

    head = jnp.concatenate([jnp.zeros((PAD_FRONT, D), x.dtype), meta_tokens.astype(x.dtype)], axis=0)
    from_x = Lp // _row_tile(Lp) > 1
    hp = x if from_x else jnp.concatenate([jnp.broadcast_to(head[None], (B, HEAD_ROWS, D)), x], axis=1)

    pos = jnp.arange(Lp, dtype=F32)
    inv = ROPE_BASE ** (-jnp.arange(0, HEAD_DIM_R, 2, dtype=F32) / HEAD_DIM_R)
    ang = pos[:, None] * inv[None]
    cc = jnp.concatenate([jnp.cos(ang), jnp.cos(ang)], axis=-1)
    ss = jnp.concatenate([-jnp.sin(ang), jnp.sin(ang)], axis=-1)
    rel = (jnp.arange(3 * BLOCK)[None, :] - BLOCK) - jnp.arange(BLOCK)[:, None]
    bucket_onehot = (_t5_bucket(rel)[None] == jnp.arange(N_BUCKETS)[:, None, None]).astype(F32)
    bias = jnp.einsum('kqs,kh->hqs', bucket_onehot, rel_bias.astype(F32), precision=lax.Precision.HIGHEST)
    bias = jnp.where((jnp.abs(rel) <= BLOCK)[None], bias * LOG2E, NEG)
    key = jnp.arange(3 * BLOCK)
    bias = jnp.stack([jnp.where(key >= BLOCK + PAD_FRONT, bias, NEG), jnp.where(key >= PAD_FRONT, bias, NEG),
                      bias, jnp.where(key < 2 * BLOCK, bias, NEG)])
    bias = bias.reshape(4, N_KV_A, 2, 2, BLOCK, 3 * BLOCK).transpose(0, 1, 2, 4, 3, 5)
    bias = bias.reshape(4, N_KV_A, 2 * BLOCK, 6 * BLOCK)
    bdq = _block_diag_mean(WA_Q, HEAD_DIM_A)
    bdk = _block_diag_mean(WA_KV, HEAD_DIM_A)

    for l in range(depth):
        qg = (jnp.tile(q_norm_g[l].astype(F32), N_HEADS_A) * (HEAD_DIM_A ** -0.5 * LOG2E))[None]
        kg = jnp.tile(k_norm_g[l].astype(F32), N_KV_A)[None]
        lg2 = -jnp.exp(ret_decay[l].astype(F32))
        first = from_x and l == 0
        q, k, v, rq, rk, rv, gf, gb = _in_proj(hp, head, first, norm1_g[l][None].astype(F32),
                                               w_in[l].astype(BF16), cc, ss, qg, kg, bdq, bdk)
        sf, sb = _ret_states(lg2, rk, rv)
        wr = jnp.pad(w_router[l].astype(F32), ((0, 0), (0, LANES - N_EXPERTS)))
        wr_hi = wr.astype(BF16)
        wr_lo = (wr - wr_hi.astype(F32)).astype(BF16)
        wr_split = jnp.concatenate([wr_hi, wr_lo], axis=1)
        h1, hn, aff = _mixer(lg2, attn_sink[l].astype(F32) * LOG2E, q, k, v, rq, rk, rv, gf, gb, sf, sb, bias,
                             hp, head, first, w_out[l].astype(BF16), norm2_g[l][None].astype(F32), wr_split)
        posm, cpos = _route(aff, cap)
        cpos_flat = cpos.reshape(-1)
        xe = _gather(cpos_flat, hn, posm, cpad)
        ye = _ffn(xe, w_gate, w_up, w_down, l)
        last = l == depth - 1
        hp = _combine(cpos_flat, h1, posm, aff, ye, cpad, skip_rows=(PAD_FRONT + N_META) if last else 0)
    return hp
```
